```python
import math
import jax, jax.numpy as jnp
from jax import lax
import numpy as np

D_MODEL = 1024
BATCH = 2
SEQ = 8192
DEPTH = 4
DEC_BATCH = 128
DEC_SEQ = 8
PAST_LEN = 2048
PAGE_SIZE = 128

N_A_LAYERS = DEPTH // 2
N_B_LAYERS = DEPTH - N_A_LAYERS
SSM_WIDTH = D_MODEL
SSM_GROUP = 16
SSM_GROUPS = SSM_WIDTH // SSM_GROUP
SSM_STATE = 64
SSM_CHUNK = 128
DT_MIN = 1e-3
DT_MAX = 1e-1
WINDOWS = (128, 512, 2048)
DILATIONS = (1, 4, 16)
N_GROUPS_B = 3
HEADS_PER_GROUP = 16
HEAD_DIM = 64
ATTN_WIDTH = HEADS_PER_GROUP * HEAD_DIM
Q_WIDTH = N_GROUPS_B * ATTN_WIDTH
ROT_DIM = HEAD_DIM // 4
ROPE_THETA = 500000.0
Q_BLOCK = 128
PLE_DIM = 256
LN_EPS = 1e-5
DN_ALPHA = (2 * DEPTH) ** 0.25
DN_BETA = (8 * DEPTH) ** -0.25
NEG_INF = -1e30

kernel_name = 'yoco_s5_dilated_swa_decoder_step'


def layer_norm(x, g, b):
    xf = x.astype(jnp.float32)
    mu = jnp.mean(xf, axis=-1, keepdims=True)
    var = jnp.mean(jnp.square(xf - mu), axis=-1, keepdims=True)
    y = (xf - mu) * lax.rsqrt(var + LN_EPS) * g.astype(jnp.float32) + b.astype(jnp.float32)
    return y.astype(x.dtype)


def partial_rotary(x, pos):
    half = ROT_DIM // 2
    inv_freq = ROPE_THETA ** (-jnp.arange(0, ROT_DIM, 2, dtype=jnp.float32) / ROT_DIM)
    ang = pos[:, None] * inv_freq[None, :]
    shape = (1, x.shape[1]) + (1,) * (x.ndim - 3) + (half,)
    cos = jnp.cos(ang).reshape(shape)
    sin = jnp.sin(ang).reshape(shape)
    xf = x.astype(jnp.float32)
    x1 = xf[..., :half]
    x2 = xf[..., half:ROT_DIM]
    out = jnp.concatenate([x1 * cos - x2 * sin, x2 * cos + x1 * sin, xf[..., ROT_DIM:]], axis=-1)
    return out.astype(x.dtype)


def _scan_combine(left, right):
    a1, b1 = left
    a2, b2 = right
    return a1 * a2, a2 * b1 + b2


def s5_ssm(u, h0, a_re, a_im, log_dt, b_re, b_im, c_re, c_im, d_skip):
    n, L, _ = u.shape
    f32 = jnp.float32
    A = lax.complex(a_re.astype(f32), a_im.astype(f32))
    dtA = jnp.exp(log_dt.astype(f32))[:, None] * A
    A_bar = jnp.exp(dtA)
    B_bar = ((A_bar - 1.0) / A)[..., None] * lax.complex(b_re.astype(f32), b_im.astype(f32))
    C = lax.complex(c_re.astype(f32), c_im.astype(f32))
    d = d_skip.astype(f32).reshape(SSM_GROUPS, SSM_GROUP)
    chunk = math.gcd(L, SSM_CHUNK)
    n_chunks = L // chunk
    a_pow = jnp.exp(jnp.arange(1, chunk + 1, dtype=f32)[:, None, None] * dtA[None])
    uc = u.astype(f32).reshape(n, n_chunks, chunk, SSM_GROUPS, SSM_GROUP).transpose(1, 0, 2, 3, 4)

    def step(h, u_blk):
        bu = jnp.einsum('nlgm,gpm->nlgp', u_blk.astype(jnp.complex64), B_bar)
        a = jnp.broadcast_to(A_bar, bu.shape)
        _, hs = lax.associative_scan(_scan_combine, (a, bu), axis=1)
        hs = hs + a_pow[None] * h[:, None]
        y = jnp.einsum('nlgp,gmp->nlgm', hs, C).real + d * u_blk
        return hs[:, -1], y

    h_last, ys = lax.scan(step, h0, uc)
    y = ys.transpose(1, 0, 2, 3, 4).reshape(n, L, SSM_WIDTH)
    return y, h_last


def mixer_a(x, h0, w_in, a_re, a_im, log_dt, b_re, b_im, c_re, c_im, d_skip, w_glu, w_out):
    f32 = jnp.float32
    uz = x @ w_in
    u = uz[..., :SSM_WIDTH]
    z = uz[..., SSM_WIDTH:]
    y, h_last = s5_ssm(u, h0, a_re, a_im, log_dt, b_re, b_im, c_re, c_im, d_skip)
    g = jax.nn.gelu(y)
    glu = g * jax.nn.sigmoid(g @ w_glu.astype(f32))
    out = (glu * jax.nn.silu(z.astype(f32))).astype(x.dtype) @ w_out
    return out, h_last


def group_attend(q, kv, q_idx, k_start, window, dil):
    j = jnp.arange(window // dil + 1)
    idx = q_idx[:, None] - j[None, :] * dil
    valid = (idx >= 0) & (idx + k_start >= 0)
    kvg = jnp.take(kv, jnp.maximum(idx, 0), axis=1)
    kg = kvg[:, :, :, 0].astype(jnp.float32)
    vg = kvg[:, :, :, 1].astype(jnp.float32)
    s = jnp.einsum('nqhd,nqkhd->nhqk', q.astype(jnp.float32), kg) * (HEAD_DIM ** -0.5)
    s = jnp.where(valid[None, None], s, NEG_INF)
    m = jnp.max(s, axis=-1)
    p = jnp.exp(s - m[..., None])
    l = jnp.sum(p, axis=-1)
    o = jnp.einsum('nhqk,nqkhd->nqhd', p, vg) / jnp.transpose(l, (0, 2, 1))[..., None]
    return o, m, l


def combine_groups(parts):
    o = jnp.stack([pt[0] for pt in parts])
    m = jnp.stack([pt[1] for pt in parts])
    l = jnp.stack([pt[2] for pt in parts])
    w = jnp.exp(m - jnp.max(m, axis=0)) * l
    alpha = jnp.transpose(w / jnp.sum(w, axis=0), (0, 1, 3, 2))[..., None]
    return jnp.sum(alpha * o, axis=0)


def dilated_attention_prompt(q, kv_groups):
    n, S = q.shape[:2]
    qs = [q[:, :, g] for g in range(N_GROUPS_B)]
    kv_pad = [jnp.pad(kv, ((0, 0), (W, 0), (0, 0), (0, 0), (0, 0))) for kv, W in zip(kv_groups, WINDOWS)]

    def block(s0):
        parts = []
        for g, (W, dil) in enumerate(zip(WINDOWS, DILATIONS)):
            qb = lax.dynamic_slice_in_dim(qs[g], s0, Q_BLOCK, axis=1)
            kvb = lax.dynamic_slice_in_dim(kv_pad[g], s0, W + Q_BLOCK, axis=1)
            parts.append(group_attend(qb, kvb, W + jnp.arange(Q_BLOCK), s0 - W, W, dil))
        return combine_groups(parts)

    out = lax.map(block, jnp.arange(S // Q_BLOCK, dtype=jnp.int32) * Q_BLOCK)
    return out.transpose(1, 0, 2, 3, 4).reshape(n, S, HEADS_PER_GROUP, HEAD_DIM)


def dilated_attention_sample(q, kv_new, caches):
    ds = q.shape[1]
    parts = []
    for g, (W, dil) in enumerate(zip(WINDOWS, DILATIONS)):
        cache = caches[g]
        lc = cache.shape[1]
        kv_all = jnp.concatenate([cache, kv_new[g]], axis=1)
        parts.append(group_attend(q[:, :, g], kv_all, lc + jnp.arange(ds), PAST_LEN - lc, W, dil))
    return combine_groups(parts)


def shared_kv(x, pos, w_kv):
    n, L, _ = x.shape
    kv = (x @ w_kv).reshape(n, L, 2, N_GROUPS_B, HEADS_PER_GROUP, HEAD_DIM)
    k = partial_rotary(kv[:, :, 0], pos)
    kv = jnp.stack([k, kv[:, :, 1]], axis=2)
    return [kv[:, :, :, g] for g in range(N_GROUPS_B)]


def mixer_b(x, pos, kv_groups, caches, w_in, w_out):
    n, L, _ = x.shape
    qz = x @ w_in
    q = partial_rotary(qz[..., :Q_WIDTH].reshape(n, L, N_GROUPS_B, HEADS_PER_GROUP, HEAD_DIM), pos)
    z = qz[..., Q_WIDTH:]
    if caches is None:
        o = dilated_attention_prompt(q, kv_groups)
    else:
        o = dilated_attention_sample(q, kv_groups, caches)
    gated = o.reshape(n, L, ATTN_WIDTH) * jax.nn.silu(z.astype(jnp.float32))
    return gated.astype(x.dtype) @ w_out


def post_layer(x, sub, p_i, g, b, w_pe_i, w_pg_i):
    h = layer_norm(DN_ALPHA * x + sub, g, b)
    gate = jax.nn.sigmoid((h @ w_pg_i).astype(jnp.float32))
    return (h.astype(jnp.float32) + gate * (p_i @ w_pe_i).astype(jnp.float32)).astype(x.dtype)


def setup_inputs(seed: int = 0) -> dict:
    key = jax.random.key(seed)
    k = jax.random.split(key, 26)
    f32 = jnp.float32

    def nrm(kk, shape, scale=1.0):
        return scale * jax.random.normal(kk, shape, f32)

    G, P, M = SSM_GROUPS, SSM_STATE, SSM_GROUP
    kv_shape = lambda W: (DEC_BATCH, min(W, PAST_LEN), 2, HEADS_PER_GROUP, HEAD_DIM)
    return {
        'x_prompt': nrm(k[0], (BATCH, SEQ, D_MODEL)),
        'x_sample': nrm(k[1], (DEC_BATCH, DEC_SEQ, D_MODEL)),
        'state_ssm_re': nrm(k[2], (N_A_LAYERS, DEC_BATCH, G, P), 0.5),
        'state_ssm_im': nrm(k[3], (N_A_LAYERS, DEC_BATCH, G, P), 0.5),
        'cache_kv_w128': nrm(k[4], kv_shape(WINDOWS[0])),
        'cache_kv_w512': nrm(k[5], kv_shape(WINDOWS[1])),
        'cache_kv_w2048': nrm(k[6], kv_shape(WINDOWS[2])),
        'p_prompt': nrm(k[7], (DEPTH, BATCH, SEQ, PLE_DIM)),
        'p_sample': nrm(k[8], (DEPTH, DEC_BATCH, DEC_SEQ, PLE_DIM)),
        'ln_g': 1.0 + nrm(k[9], (DEPTH, D_MODEL), 0.02),
        'ln_b': nrm(k[10], (DEPTH, D_MODEL), 0.02),
        'w_pe': nrm(k[11], (DEPTH, PLE_DIM, D_MODEL), PLE_DIM ** -0.5),
        'w_pg': nrm(k[12], (DEPTH, D_MODEL, D_MODEL), D_MODEL ** -0.5),
        'w_in_a': nrm(k[13], (N_A_LAYERS, D_MODEL, 2 * SSM_WIDTH), D_MODEL ** -0.5),
        'a_re': -0.5 + nrm(k[14], (N_A_LAYERS, G, P), 0.01),
        'a_im': jnp.broadcast_to(jnp.pi * jnp.arange(P, dtype=f32), (N_A_LAYERS, G, P)),
        'log_dt': jax.random.uniform(k[15], (N_A_LAYERS, G), f32, minval=math.log(DT_MIN), maxval=math.log(DT_MAX)),
        'b_re': nrm(k[16], (N_A_LAYERS, G, P, M), (2 * M) ** -0.5),
        'b_im': nrm(k[17], (N_A_LAYERS, G, P, M), (2 * M) ** -0.5),
        'c_re': nrm(k[18], (N_A_LAYERS, G, M, P), P ** -0.5),
        'c_im': nrm(k[19], (N_A_LAYERS, G, M, P), P ** -0.5),
        'd_skip': nrm(k[20], (N_A_LAYERS, SSM_WIDTH)),
        'w_glu': nrm(k[21], (N_A_LAYERS, SSM_WIDTH, SSM_WIDTH), SSM_WIDTH ** -0.5),
        'w_out_a': nrm(k[22], (N_A_LAYERS, SSM_WIDTH, D_MODEL), DN_BETA * SSM_WIDTH ** -0.5),
        'w_kv': nrm(k[23], (D_MODEL, 2 * Q_WIDTH), D_MODEL ** -0.5),
        'w_in_b': nrm(k[24], (N_B_LAYERS, D_MODEL, Q_WIDTH + ATTN_WIDTH), D_MODEL ** -0.5),
        'w_out_b': nrm(k[25], (N_B_LAYERS, ATTN_WIDTH, D_MODEL), DN_BETA * ATTN_WIDTH ** -0.5),
    }


def reference(x_prompt, x_sample, state_ssm_re, state_ssm_im, cache_kv_w128, cache_kv_w512, cache_kv_w2048,
              p_prompt, p_sample, ln_g, ln_b, w_pe, w_pg, w_in_a, a_re, a_im, log_dt, b_re, b_im, c_re, c_im,
              d_skip, w_glu, w_out_a, w_kv, w_in_b, w_out_b):
    f32 = jnp.float32

    def run(x, p, h0, caches, pos):
        h_last = []
        kv_groups = None
        for i in range(DEPTH):
            if i < N_A_LAYERS:
                sub, hl = mixer_a(x, h0[i], w_in_a[i], a_re[i], a_im[i], log_dt[i], b_re[i], b_im[i],
                                  c_re[i], c_im[i], d_skip[i], w_glu[i], w_out_a[i])
                h_last.append(hl)
            else:
                if kv_groups is None:
                    kv_groups = shared_kv(x, pos, w_kv)
                j = i - N_A_LAYERS
                sub = mixer_b(x, pos, kv_groups, caches, w_in_b[j], w_out_b[j])
            x = post_layer(x, sub, p[i], ln_g[i], ln_b[i], w_pe[i], w_pg[i])
        return x, jnp.stack(h_last), kv_groups

    seq = x_prompt.shape[1]
    pos_p = jnp.arange(seq, dtype=f32)
    pos_s = PAST_LEN + jnp.arange(x_sample.shape[1], dtype=f32)
    h0_p = jnp.zeros((N_A_LAYERS, x_prompt.shape[0], SSM_GROUPS, SSM_STATE), jnp.complex64)
    h0_s = lax.complex(state_ssm_re.astype(f32), state_ssm_im.astype(f32))

    y_prompt, h_p, kv_p = run(x_prompt, p_prompt, h0_p, None, pos_p)
    y_sample, h_s, kv_s = run(x_sample, p_sample, h0_s, (cache_kv_w128, cache_kv_w512, cache_kv_w2048), pos_s)

    kv_w128_prompt = kv_p[0][:, seq - min(WINDOWS[0], seq):]
    kv_w512_prompt = kv_p[1][:, seq - min(WINDOWS[1], seq):]
    kv_w2048_prompt = kv_p[2][:, seq - min(WINDOWS[2], seq):]
    kv_w128_sample = kv_s[0]
    kv_w512_sample = kv_s[1]
    kv_w2048_sample = kv_s[2]
    return (y_prompt, y_sample, h_p.real, h_p.imag, h_s.real, h_s.imag,
            kv_w128_prompt, kv_w512_prompt, kv_w2048_prompt,
            kv_w128_sample, kv_w512_sample, kv_w2048_sample)
```

```python
import functools
import math

import jax
import jax.numpy as jnp
from jax import lax
from jax.experimental import pallas as pl
from jax.experimental.pallas import tpu as pltpu

F32 = jnp.float32
BF16 = jnp.bfloat16

D_MODEL = 1024
N_SSM_GROUPS = 64
SSM_GROUP = 16
SSM_STATE = 64
SSM_CHUNK = 16
N_ATT_GROUPS = 3
WINDOWS = (128, 512, 2048)
DILATIONS = (1, 4, 16)
HEAD_DIM = 64
N_HEADS = 16
ATTN_WIDTH = N_HEADS * HEAD_DIM
ROT_DIM = HEAD_DIM // 4
ROPE_THETA = 500000.0
WINDOW_KEYS = 128
LN_EPS = 1e-5
NEG_INF = -1e30
LANES = 128
SUBLANES = 8
VMEM_LIMIT = 52 * 1024 * 1024


def _sigmoid(x):
    return 1.0 / (1.0 + jnp.exp(-x))


def _gelu_tanh(x):
    return 0.5 * x * (1.0 + jnp.tanh(math.sqrt(2.0 / math.pi) * (x + 0.044715 * (x * x * x))))


def _mm_kernel(x_ref, w_ref, o_ref):
    o_ref[...] = jnp.dot(x_ref[...], w_ref[...], preferred_element_type=F32).astype(o_ref.dtype)


def _mm_rope_kernel(x_ref, w_ref, cos_ref, sa_ref, sb_ref, o_ref, *, rot_blocks):
    j = pl.program_id(1)
    acc = jnp.dot(x_ref[...], w_ref[...], preferred_element_type=F32)
    tn = acc.shape[1]

    @pl.when(j < rot_blocks)
    def _():
        reps = tn // LANES
        cos = jnp.concatenate([cos_ref[...]] * reps, axis=1)
        sa = jnp.concatenate([sa_ref[...]] * reps, axis=1)
        sb = jnp.concatenate([sb_ref[...]] * reps, axis=1)
        half = ROT_DIM // 2
        rot = acc * cos + pltpu.roll(acc, tn - half, 1) * sa + pltpu.roll(acc, half, 1) * sb
        o_ref[...] = rot.astype(o_ref.dtype)

    @pl.when(j >= rot_blocks)
    def _():
        o_ref[...] = acc.astype(o_ref.dtype)


def _token_tile(n_tokens, want):
    tm = min(want, n_tokens)
    assert n_tokens % tm == 0
    return tm


def _project(x, w, *, rope=None, rot_cols=0, tn=512, out_dtype=BF16):
    t, k = x.shape
    n = w.shape[1]
    tm = _token_tile(t, 1024)
    grid = (t // tm, n // tn)
    x_spec = pl.BlockSpec((tm, k), lambda i, j: (i, 0))
    w_spec = pl.BlockSpec((k, tn), lambda i, j: (0, j))
    o_spec = pl.BlockSpec((tm, tn), lambda i, j: (i, j))
    params = pltpu.CompilerParams(dimension_semantics=("parallel", "arbitrary"), vmem_limit_bytes=VMEM_LIMIT)
    if rope is None:
        return pl.pallas_call(
            _mm_kernel, grid=grid, in_specs=[x_spec, w_spec], out_specs=o_spec,
            out_shape=jax.ShapeDtypeStruct((t, n), out_dtype), compiler_params=params, name="proj",
        )(x, w)
    assert rot_cols % tn == 0
    tab_spec = pl.BlockSpec((tm, LANES), lambda i, j: (i, 0))
    return pl.pallas_call(
        functools.partial(_mm_rope_kernel, rot_blocks=rot_cols // tn), grid=grid,
        in_specs=[x_spec, w_spec, tab_spec, tab_spec, tab_spec], out_specs=o_spec,
        out_shape=jax.ShapeDtypeStruct((t, n), out_dtype), compiler_params=params, name="proj_rope",
    )(x, w, *rope)


def _rope_tables(pos, scale):
    half = ROT_DIM // 2
    inv_freq = ROPE_THETA ** (-jnp.arange(0, ROT_DIM, 2, dtype=F32) / ROT_DIM)
    ang = pos[:, None] * inv_freq[None, :]
    cos, sin = jnp.cos(ang), jnp.sin(ang)
    t = pos.shape[0]
    ones = jnp.ones((t, HEAD_DIM - ROT_DIM), F32)
    zeros = jnp.zeros((t, HEAD_DIM - ROT_DIM), F32)
    zhalf = jnp.zeros((t, half), F32)
    cos64 = jnp.concatenate([cos, cos, ones], axis=1)
    sa64 = jnp.concatenate([-sin, zhalf, zeros], axis=1)
    sb64 = jnp.concatenate([zhalf, sin, zeros], axis=1)
    return tuple(scale * jnp.concatenate([a, a], axis=1) for a in (cos64, sa64, sb64))


def _ssm_operators(a_re, a_im, log_dt, b_re, b_im, c_re, c_im, d_skip):
    g, p, m, c = N_SSM_GROUPS, SSM_STATE, SSM_GROUP, SSM_CHUNK
    hi = lax.Precision.HIGHEST
    a = lax.complex(a_re.astype(F32), a_im.astype(F32))
    dta = jnp.exp(log_dt.astype(F32))[:, None] * a
    a_bar = jnp.exp(dta)
    b_bar = ((a_bar - 1.0) / a)[..., None] * lax.complex(b_re.astype(F32), b_im.astype(F32))
    cc = lax.complex(c_re.astype(F32), c_im.astype(F32))
    pw = jnp.exp(jnp.arange(c + 1, dtype=F32)[:, None, None] * dta[None])

    cp = cc[None] * pw[:c, :, None, :]
    kern = (jnp.einsum("tgmp,gpn->tgmn", cp.real, b_bar.real, precision=hi)
            - jnp.einsum("tgmp,gpn->tgmn", cp.imag, b_bar.imag, precision=hi))
    s_idx = jnp.arange(c)[:, None]
    t_idx = jnp.arange(c)[None, :]
    lag = t_idx - s_idx
    blocks = kern[jnp.clip(lag, 0, c - 1)]
    blocks = jnp.where((lag >= 0)[:, :, None, None, None], blocks, 0.0)
    t_op = blocks.transpose(2, 0, 4, 1, 3).reshape(g, c * m, c * m)
    d_rep = jnp.tile(d_skip.astype(F32).reshape(g, m), (1, c))
    t_op = t_op + jnp.eye(c * m, dtype=F32)[None] * d_rep[:, None, :]

    bin_c = pw[c - 1 - jnp.arange(c)][:, :, :, None] * b_bar[None]
    bin_c = bin_c.transpose(1, 0, 3, 2).reshape(g, c * m, p)
    b_in = jnp.concatenate([bin_c.real, bin_c.imag], axis=-1)

    w = cc[None] * pw[1:, :, None, :]
    w = w.transpose(1, 3, 0, 2).reshape(g, p, c * m)
    c_out = jnp.concatenate([w.real, -w.imag], axis=1)

    steps = jnp.array([16, 32, 64, 128, 8, 0, 0, 0] + [16 * r for r in range(SUBLANES)], F32)
    pows = jnp.exp(steps[None, :, None] * dta[:, None, :])
    ca = jnp.concatenate([pows.real, pows.real], axis=-1)
    cb = jnp.concatenate([-pows.imag, pows.imag], axis=-1)
    return t_op.astype(BF16), b_in.astype(BF16), c_out.astype(BF16), ca, cb


def _swap_halves(x):
    return pltpu.roll(x, SSM_STATE, x.ndim - 1)


def _cmul(x, xs, ca, cb):
    return x * ca + xs * cb


def _ssm_kernel(x_ref, xs_ref, h0_ref, t_ref, bin_ref, cout_ref, ca_ref, cb_ref,
                y_ref, hl_ref, ys_ref, hs_ref, l_scr, e_scr, *, n_seq, rows_per_seq):
    rows = n_seq * rows_per_seq
    n_blocks = rows_per_seq // SUBLANES
    ca = ca_ref[0]
    cb = cb_ref[0]
    x = x_ref[0]
    v = jnp.dot(x, bin_ref[0], preferred_element_type=F32)
    rmod = lax.broadcasted_iota(jnp.int32, (rows, LANES), 0) & (SUBLANES - 1)

    for k, dist in enumerate((1, 2, 4)):
        sh = jnp.where(rmod >= dist, pltpu.roll(v, dist, 0), 0.0)
        v = v + _cmul(sh, _swap_halves(sh), ca[k:k + 1], cb[k:k + 1])
    l_scr[...] = v

    ca128 = ca[3:4]
    cb128 = cb[3:4]
    cb128s = _swap_halves(cb128)

    def body(b, carry):
        e, es = carry
        lasts = []
        for s in range(n_seq):
            base = s * rows_per_seq + b * SUBLANES
            e_scr[pl.ds(base, SUBLANES), :] = jnp.broadcast_to(e[s:s + 1], (SUBLANES, LANES))
            lasts.append(l_scr[pl.ds(base + SUBLANES - 1, 1), :])
        last = jnp.concatenate(lasts, axis=0)
        e_new = last + e * ca128 + es * cb128
        es_new = _swap_halves(last) + es * ca128 + e * cb128s
        return e_new, es_new

    zero = jnp.zeros((n_seq, LANES), F32)
    e_fin, _ = lax.fori_loop(0, n_blocks, body, (zero, zero))
    hl_ref[0] = e_fin

    lshift = jnp.where(rmod >= 1, pltpu.roll(v, 1, 0), 0.0)
    ebc = e_scr[...]
    pwa = ca[SUBLANES:2 * SUBLANES]
    pwb = cb[SUBLANES:2 * SUBLANES]
    shape3 = (rows // SUBLANES, SUBLANES, LANES)
    h = (lshift.reshape(shape3) + ebc.reshape(shape3) * pwa[None]
         + _swap_halves(ebc).reshape(shape3) * pwb[None]).reshape(rows, LANES)
    y = jnp.dot(x, t_ref[0], preferred_element_type=F32)
    y = y + jnp.dot(h.astype(BF16), cout_ref[0], preferred_element_type=F32)
    y_ref[0] = y.astype(y_ref.dtype)

    half = SSM_CHUNK * SSM_GROUP // 2
    xs = xs_ref[0]
    h0 = h0_ref[0]
    ys = jnp.dot(xs, t_ref[0, :half, :half], preferred_element_type=F32)
    ys = ys + jnp.dot(h0.astype(BF16), cout_ref[0, :, :half], preferred_element_type=F32)
    ys_ref[0] = ys.astype(ys_ref.dtype)
    hs = _cmul(h0, _swap_halves(h0), ca[4:5], cb[4:5])
    hs_ref[0] = hs + jnp.dot(xs, bin_ref[0, half:, :], preferred_element_type=F32)


def _ssm(xg, xsg, h0g, ops, *, n_seq):
    t_op, b_in, c_out, ca, cb = ops
    g, rows, cw = xg.shape
    n_s = xsg.shape[1]
    hw = xsg.shape[2]

    def spec(shape):
        return pl.BlockSpec((1,) + shape, lambda i: (i, 0, 0))

    return pl.pallas_call(
        functools.partial(_ssm_kernel, n_seq=n_seq, rows_per_seq=rows // n_seq),
        grid=(g,),
        in_specs=[spec((rows, cw)), spec((n_s, hw)), spec((n_s, LANES)), spec((cw, cw)), spec((cw, LANES)),
                  spec((LANES, cw)), spec((2 * SUBLANES, LANES)), spec((2 * SUBLANES, LANES))],
        out_specs=[spec((rows, cw)), spec((n_seq, LANES)), spec((n_s, hw)), spec((n_s, LANES))],
        out_shape=[jax.ShapeDtypeStruct((g, rows, cw), BF16), jax.ShapeDtypeStruct((g, n_seq, LANES), F32),
                   jax.ShapeDtypeStruct((g, n_s, hw), BF16), jax.ShapeDtypeStruct((g, n_s, LANES), F32)],
        scratch_shapes=[pltpu.VMEM((rows, LANES), F32), pltpu.VMEM((rows, LANES), F32)],
        compiler_params=pltpu.CompilerParams(dimension_semantics=("parallel",), vmem_limit_bytes=VMEM_LIMIT),
        name="s5_scan",
    )(xg, xsg, h0g, t_op, b_in, c_out, ca, cb)


def _post_layer(x, sub, p, g_ref, b_ref, wpg_ref, wpe_ref, alpha):
    r = alpha * x + sub
    mu = jnp.mean(r, axis=-1, keepdims=True)
    cen = r - mu
    var = jnp.mean(cen * cen, axis=-1, keepdims=True)
    h = cen * lax.rsqrt(var + LN_EPS) * g_ref[...] + b_ref[...]
    gate = _sigmoid(jnp.dot(h.astype(BF16), wpg_ref[...], preferred_element_type=F32))
    ple = jnp.dot(p.astype(BF16), wpe_ref[...], preferred_element_type=F32)
    return h + gate * ple


def _glu_post_kernel(y_ref, z_ref, x_ref, p_ref, g_ref, b_ref, wglu_ref, wout_ref, wpg_ref, wpe_ref,
                     xo_ref, xb_ref, *, alpha):
    g = _gelu_tanh(y_ref[...].astype(F32))
    t = jnp.dot(g.astype(BF16), wglu_ref[...], preferred_element_type=F32)
    z = z_ref[...].astype(F32)
    glu = g * _sigmoid(t) * (z * _sigmoid(z))
    sub = jnp.dot(glu.astype(BF16), wout_ref[...], preferred_element_type=F32)
    out = _post_layer(x_ref[...], sub, p_ref[...], g_ref, b_ref, wpg_ref, wpe_ref, alpha)
    xo_ref[...] = out
    xb_ref[...] = out.astype(BF16)


def _comb_post_kernel(*refs, n_groups, alpha):
    o_refs = refs[:n_groups]
    lse_refs = refs[n_groups:2 * n_groups] if n_groups > 1 else ()
    k = len(o_refs) + len(lse_refs)
    z_ref, x_ref, p_ref, g_ref, b_ref, wout_ref, wpg_ref, wpe_ref, xo_ref, xb_ref = refs[k:]
    if n_groups == 1:
        o = o_refs[0][...].astype(F32)
    else:
        lses = [r[...] for r in lse_refs]
        top = functools.reduce(jnp.maximum, lses)
        ws = [jnp.exp(l - top) for l in lses]
        num = functools.reduce(lambda a, c: a + c, [w * r[...].astype(F32) for w, r in zip(ws, o_refs)])
        o = num / functools.reduce(lambda a, c: a + c, ws)
    z = z_ref[...].astype(F32)
    gated = o * (z * _sigmoid(z))
    sub = jnp.dot(gated.astype(BF16), wout_ref[...], preferred_element_type=F32)
    out = _post_layer(x_ref[...], sub, p_ref[...], g_ref, b_ref, wpg_ref, wpe_ref, alpha)
    xo_ref[...] = out
    xb_ref[...] = out.astype(BF16)


def _tok_spec(tm, width, col=0):
    return pl.BlockSpec((tm, width), lambda i: (i, col))


def _full_spec(shape):
    return pl.BlockSpec(shape, lambda i: (0,) * len(shape))


def _post_outputs(t, tm):
    return dict(
        out_specs=[_tok_spec(tm, D_MODEL), _tok_spec(tm, D_MODEL)],
        out_shape=[jax.ShapeDtypeStruct((t, D_MODEL), F32), jax.ShapeDtypeStruct((t, D_MODEL), BF16)],
        compiler_params=pltpu.CompilerParams(dimension_semantics=("parallel",), vmem_limit_bytes=VMEM_LIMIT),
    )


def _glu_post(y, uz, x, p, ln_g, ln_b, w_glu, w_out, w_pg, w_pe, alpha):
    t = x.shape[0]
    tm = _token_tile(t, 512)
    ple = p.shape[1]
    return pl.pallas_call(
        functools.partial(_glu_post_kernel, alpha=alpha), grid=(t // tm,),
        in_specs=[_tok_spec(tm, D_MODEL), _tok_spec(tm, D_MODEL, 1), _tok_spec(tm, D_MODEL), _tok_spec(tm, ple),
                  _full_spec((1, D_MODEL)), _full_spec((1, D_MODEL)), _full_spec((D_MODEL, D_MODEL)),
                  _full_spec((D_MODEL, D_MODEL)), _full_spec((D_MODEL, D_MODEL)), _full_spec((ple, D_MODEL))],
        name="glu_post", **_post_outputs(t, tm),
    )(y, uz, x, p, ln_g, ln_b, w_glu, w_out, w_pg, w_pe)


def _comb_post(os_, lses, qz, x, p, ln_g, ln_b, w_out, w_pg, w_pe, alpha):
    t = x.shape[0]
    tm = _token_tile(t, 512)
    ple = p.shape[1]
    n_groups = len(os_)
    z_col = qz.shape[1] // D_MODEL - 1
    return pl.pallas_call(
        functools.partial(_comb_post_kernel, n_groups=n_groups, alpha=alpha), grid=(t // tm,),
        in_specs=[_tok_spec(tm, D_MODEL)] * (n_groups + len(lses))
        + [_tok_spec(tm, D_MODEL, z_col), _tok_spec(tm, D_MODEL), _tok_spec(tm, ple),
           _full_spec((1, D_MODEL)), _full_spec((1, D_MODEL)), _full_spec((D_MODEL, D_MODEL)),
           _full_spec((D_MODEL, D_MODEL)), _full_spec((ple, D_MODEL))],
        name="attn_out_post", **_post_outputs(t, tm),
    )(*os_, *lses, qz, x, p, ln_g, ln_b, w_out, w_pg, w_pe)


def _attn_prompt_kernel(q_ref, kp_ref, kc_ref, vp_ref, vc_ref, o_ref, lse_ref):
    ib = pl.program_id(2)
    tq = q_ref.shape[0]
    row = lax.broadcasted_iota(jnp.int32, (tq, 2 * tq), 0)
    col = lax.broadcasted_iota(jnp.int32, (tq, 2 * tq), 1)
    back = row + tq - col
    valid = (back >= 0) & (back <= WINDOW_KEYS) & ((col >= tq) | (ib > 0))
    low = lax.broadcasted_iota(jnp.int32, (tq, LANES), 1) < HEAD_DIM
    nt = (((1,), (1,)), ((), ()))
    for j in range(ATTN_WIDTH // LANES):
        sl = slice(j * LANES, (j + 1) * LANES)
        q2 = q_ref[:, sl]
        k2 = jnp.concatenate([kp_ref[:, sl], kc_ref[:, sl]], axis=0)
        v2 = jnp.concatenate([vp_ref[:, sl], vc_ref[:, sl]], axis=0)
        outs = []
        for sel in (low, jnp.logical_not(low)):
            qm = jnp.where(sel, q2, jnp.zeros_like(q2))
            s = lax.dot_general(qm, k2, nt, preferred_element_type=F32)
            s = jnp.where(valid, s, NEG_INF)
            m = jnp.max(s, axis=1, keepdims=True)
            pr = jnp.exp(s - m)
            l = jnp.sum(pr, axis=1, keepdims=True)
            o = jnp.dot(pr.astype(BF16), v2, preferred_element_type=F32)
            outs.append((o / l, m + jnp.log(l)))
        o_ref[:, sl] = jnp.where(low, outs[0][0], outs[1][0]).astype(o_ref.dtype)
        lse_ref[:, sl] = jnp.where(low, outs[0][1], outs[1][1])


def _attn_prompt_group(qz, kv, group, n_seq, seq_len):
    dil = DILATIONS[group]
    tq = WINDOW_KEYS
    t_all = qz.shape[0]
    qcols = qz.shape[1] // D_MODEL
    kcols = kv.shape[1] // D_MODEL
    qv = qz.reshape(t_all // dil, dil * qz.shape[1])
    kvv = kv.reshape(t_all // dil, dil * kv.shape[1])
    nb = seq_len // dil // tq
    t_p = n_seq * seq_len

    def cur(b, r, ib):
        return b * nb + ib

    def prev(b, r, ib):
        return b * nb + jnp.maximum(ib - 1, 0)

    blk = (tq, D_MODEL)
    o, lse = pl.pallas_call(
        _attn_prompt_kernel, grid=(n_seq, dil, nb),
        in_specs=[pl.BlockSpec(blk, lambda b, r, ib: (cur(b, r, ib), r * qcols + group)),
                  pl.BlockSpec(blk, lambda b, r, ib: (prev(b, r, ib), r * kcols + group)),
                  pl.BlockSpec(blk, lambda b, r, ib: (cur(b, r, ib), r * kcols + group)),
                  pl.BlockSpec(blk, lambda b, r, ib: (prev(b, r, ib), r * kcols + N_ATT_GROUPS + group)),
                  pl.BlockSpec(blk, lambda b, r, ib: (cur(b, r, ib), r * kcols + N_ATT_GROUPS + group))],
        out_specs=[pl.BlockSpec(blk, lambda b, r, ib: (cur(b, r, ib), r)),
                   pl.BlockSpec(blk, lambda b, r, ib: (cur(b, r, ib), r))],
        out_shape=[jax.ShapeDtypeStruct((t_p // dil, dil * D_MODEL), BF16),
                   jax.ShapeDtypeStruct((t_p // dil, dil * D_MODEL), F32)],
        compiler_params=pltpu.CompilerParams(dimension_semantics=("parallel", "parallel", "arbitrary"),
                                             vmem_limit_bytes=VMEM_LIMIT),
        name=f"attn_prompt_g{group}",
    )(qv, kvv, kvv, kvv, kvv)
    return o.reshape(t_p, D_MODEL), lse.reshape(t_p, D_MODEL)


def _attn_sample_kernel(q_ref, kvn_ref, c0_ref, c1_ref, c2_ref, o_ref, s_scr):
    ds = q_ref.shape[1]
    n_rows = N_HEADS * ds
    kv_w = 2 * ATTN_WIDTH
    rho = lax.broadcasted_iota(jnp.int32, (n_rows, ATTN_WIDTH), 0)
    lane = lax.broadcasted_iota(jnp.int32, (n_rows, ATTN_WIDTH), 1)
    assert ds & (ds - 1) == 0
    head_mask = (rho >> (ds.bit_length() - 1)) == (lane >> (HEAD_DIM.bit_length() - 1))
    qi = lax.broadcasted_iota(jnp.int32, (n_rows, WINDOW_KEYS), 0) & (ds - 1)
    kk = lax.broadcasted_iota(jnp.int32, (n_rows, WINDOW_KEYS), 1)
    nt = (((1,), (1,)), ((), ()))

    q_all = q_ref[0].astype(F32)
    kvn = kvn_ref[0].astype(F32)
    q_rows = []
    for g in range(N_ATT_GROUPS):
        qg = q_all[:, g * ATTN_WIDTH:(g + 1) * ATTN_WIDTH]
        q_rows.append(jnp.where(head_mask, jnp.concatenate([qg] * N_HEADS, axis=0), 0.0).astype(BF16))

    def new_block(g, off):
        piece = kvn[:, off + g * ATTN_WIDTH: off + (g + 1) * ATTN_WIDTH]
        pad = jnp.zeros((WINDOW_KEYS - ds, ATTN_WIDTH), F32)
        return jnp.concatenate([piece, pad], axis=0).astype(BF16)

    slabs = []
    slabs.append((0, lambda: c0_ref[0, :, 0:ATTN_WIDTH], lambda: c0_ref[0, :, ATTN_WIDTH:kv_w], kk >= qi))
    for r in range(DILATIONS[1]):
        msk = (qi == r) | ((qi == r + DILATIONS[1]) & (kk >= 1))
        slabs.append((1, functools.partial(lambda r: c1_ref[0, :, r * kv_w: r * kv_w + ATTN_WIDTH], r),
                      functools.partial(lambda r: c1_ref[0, :, r * kv_w + ATTN_WIDTH:(r + 1) * kv_w], r), msk))
    for r in range(ds):
        slabs.append((2, functools.partial(lambda r: c2_ref[0, :, r * kv_w: r * kv_w + ATTN_WIDTH], r),
                      functools.partial(lambda r: c2_ref[0, :, r * kv_w + ATTN_WIDTH:(r + 1) * kv_w], r), qi == r))
    new_masks = (kk <= qi, (kk == qi) | (kk == qi - DILATIONS[1]), kk == qi)
    for g in range(N_ATT_GROUPS):
        slabs.append((g, functools.partial(new_block, g, 0),
                      functools.partial(new_block, g, N_ATT_GROUPS * ATTN_WIDTH), new_masks[g] & (kk < ds)))

    m = jnp.full((n_rows, 1), NEG_INF, F32)
    for t, (g, load_k, _, msk) in enumerate(slabs):
        s = lax.dot_general(q_rows[g], load_k().astype(BF16), nt, preferred_element_type=F32)
        s = jnp.where(msk, s, NEG_INF)
        s_scr[t] = s
        m = jnp.maximum(m, jnp.max(s, axis=1, keepdims=True))

    l = jnp.zeros((n_rows, 1), F32)
    acc = jnp.zeros((n_rows, ATTN_WIDTH), F32)
    for t, (g, _, load_v, msk) in enumerate(slabs):
        pr = jnp.where(msk, jnp.exp(s_scr[t] - m), 0.0)
        l = l + jnp.sum(pr, axis=1, keepdims=True)
        acc = acc + jnp.dot(pr.astype(BF16), load_v().astype(BF16), preferred_element_type=F32)
    acc = acc / l

    acc3 = acc.reshape(N_HEADS, ds, ATTN_WIDTH)
    low = lax.broadcasted_iota(jnp.int32, (ds, LANES), 1) < HEAD_DIM
    for c in range(ATTN_WIDTH // LANES):
        sl = slice(c * LANES, (c + 1) * LANES)
        o_ref[0, :, sl] = jnp.where(low, acc3[2 * c][:, sl], acc3[2 * c + 1][:, sl])


def _attn_sample(qz_s, kv_s, caches, n_seq, ds):
    q3 = qz_s.reshape(n_seq, ds, qz_s.shape[1])
    kv3 = kv_s.reshape(n_seq, ds, kv_s.shape[1])
    views = []
    for cache, dil in zip(caches, DILATIONS):
        assert cache.shape[1] == WINDOW_KEYS * dil
        views.append(cache.reshape(n_seq, WINDOW_KEYS, dil * 2 * ATTN_WIDTH))
    n_slabs = 1 + DILATIONS[1] + ds + N_ATT_GROUPS
    widths = (2 * ATTN_WIDTH, DILATIONS[1] * 2 * ATTN_WIDTH, ds * 2 * ATTN_WIDTH)

    def spec(shape):
        return pl.BlockSpec((1,) + shape, lambda n: (n, 0, 0))

    o = pl.pallas_call(
        _attn_sample_kernel, grid=(n_seq,),
        in_specs=[spec((ds, q3.shape[2])), spec((ds, kv3.shape[2]))] + [spec((WINDOW_KEYS, w)) for w in widths],
        out_specs=spec((ds, ATTN_WIDTH)),
        out_shape=jax.ShapeDtypeStruct((n_seq, ds, ATTN_WIDTH), F32),
        scratch_shapes=[pltpu.VMEM((n_slabs, N_HEADS * ds, WINDOW_KEYS), F32)],
        compiler_params=pltpu.CompilerParams(dimension_semantics=("parallel",), vmem_limit_bytes=VMEM_LIMIT),
        name="attn_sample",
    )(q3, kv3, *views)
    return o.reshape(n_seq * ds, ATTN_WIDTH)


def kernel(x_prompt, x_sample, state_ssm_re, state_ssm_im, cache_kv_w128, cache_kv_w512, cache_kv_w2048, p_prompt, p_sample, ln_g, ln_b, w_pe, w_pg, w_in_a, a_re, a_im, log_dt, b_re, b_im, c_re, c_im, d_skip, w_glu, w_out_a, w_kv, w_in_b, w_out_b):
    depth = ln_g.shape[0]
    n_a = w_in_a.shape[0]
    batch, seq, _ = x_prompt.shape
    n_s, ds, _ = x_sample.shape
    past = cache_kv_w2048.shape[1]
    assert ds * 2 == SSM_CHUNK and ds <= DILATIONS[2] and past == WINDOWS[2] and seq % (DILATIONS[2] * WINDOW_KEYS) == 0
    alpha = (2 * depth) ** 0.25
    t_p, t_s = batch * seq, n_s * ds
    g, cw = N_SSM_GROUPS, SSM_CHUNK * SSM_GROUP
    caches = (cache_kv_w128, cache_kv_w512, cache_kv_w2048)

    xs = [x_prompt.reshape(t_p, D_MODEL), x_sample.reshape(t_s, D_MODEL)]
    xbs = [v.astype(BF16) for v in xs]
    ps = [p_prompt.reshape(depth, t_p, -1), p_sample.reshape(depth, t_s, -1)]
    wb = lambda w: w.astype(BF16)
    h_prompt, h_sample = [], []

    for i in range(n_a):
        uzs = [_project(xb, wb(w_in_a[i]), tn=512) for xb in xbs]
        ops = _ssm_operators(a_re[i], a_im[i], log_dt[i], b_re[i], b_im[i], c_re[i], c_im[i], d_skip[i])
        xg = uzs[0][:, :D_MODEL].reshape(t_p // SSM_CHUNK, SSM_CHUNK, g, SSM_GROUP)
        xg = xg.transpose(2, 0, 1, 3).reshape(g, t_p // SSM_CHUNK, cw)
        xsg = uzs[1][:, :D_MODEL].reshape(n_s, ds, g, SSM_GROUP).transpose(2, 0, 1, 3).reshape(g, n_s, cw // 2)
        h0g = jnp.concatenate([state_ssm_re[i], state_ssm_im[i]], axis=-1).astype(F32).transpose(1, 0, 2)
        yg, hl, ysg, hs = _ssm(xg, xsg, h0g, ops, n_seq=batch)
        y_p = yg.reshape(g, t_p // SSM_CHUNK, SSM_CHUNK, SSM_GROUP).transpose(1, 2, 0, 3).reshape(t_p, D_MODEL)
        y_s = ysg.reshape(g, n_s, ds, SSM_GROUP).transpose(1, 2, 0, 3).reshape(t_s, D_MODEL)
        h_prompt.append(hl.transpose(1, 0, 2))
        h_sample.append(hs.transpose(1, 0, 2))
        for k, y in enumerate((y_p, y_s)):
            xs[k], xbs[k] = _glu_post(y, uzs[k], xs[k], ps[k][i], ln_g[i:i + 1], ln_b[i:i + 1], wb(w_glu[i]),
                                      wb(w_out_a[i]), wb(w_pg[i]), wb(w_pe[i]), alpha)

    pos = [jnp.tile(jnp.arange(seq, dtype=F32), batch), jnp.tile(past + jnp.arange(ds, dtype=F32), n_s)]
    rope_k = [_rope_tables(p_, 1.0) for p_ in pos]
    rope_q = [_rope_tables(p_, HEAD_DIM ** -0.5) for p_ in pos]
    q_width = N_ATT_GROUPS * ATTN_WIDTH
    kvs = [_project(xb, wb(w_kv), rope=rk, rot_cols=q_width, tn=512) for xb, rk in zip(xbs, rope_k)]

    for i in range(n_a, depth):
        j = i - n_a
        qzs = [_project(xb, wb(w_in_b[j]), rope=rq, rot_cols=q_width, tn=512) for xb, rq in zip(xbs, rope_q)]
        parts = [_attn_prompt_group(qzs[0], kvs[0], grp, batch, seq) for grp in range(N_ATT_GROUPS)]
        o_s = _attn_sample(qzs[1], kvs[1], caches, n_s, ds)
        attn = [([pt[0] for pt in parts], [pt[1] for pt in parts]), ([o_s], [])]
        for k in range(2):
            xs[k], xbs[k] = _comb_post(attn[k][0], attn[k][1], qzs[k], xs[k], ps[k][i], ln_g[i:i + 1], ln_b[i:i + 1],
                                       wb(w_out_b[j]), wb(w_pg[i]), wb(w_pe[i]), alpha)

    y_prompt = xs[0].reshape(batch, seq, D_MODEL)
    y_sample = xs[1].reshape(n_s, ds, D_MODEL)
    hp = jnp.stack(h_prompt)
    hsm = jnp.stack(h_sample)
    kv_p = kvs[0].astype(F32).reshape(batch, seq, 2, N_ATT_GROUPS, N_HEADS, HEAD_DIM)
    kv_s = kvs[1].astype(F32).reshape(n_s, ds, 2, N_ATT_GROUPS, N_HEADS, HEAD_DIM)
    kv_prompt = [kv_p[:, seq - min(w, seq):, :, grp] for grp, w in enumerate(WINDOWS)]
    kv_sample = [kv_s[:, :, :, grp] for grp in range(N_ATT_GROUPS)]
    return (y_prompt, y_sample, hp[..., :SSM_STATE], hp[..., SSM_STATE:], hsm[..., :SSM_STATE], hsm[..., SSM_STATE:],
            *kv_prompt, *kv_sample)
```

```python
import functools
import math

import jax
import jax.numpy as jnp
from jax import lax
from jax.experimental import pallas as pl
from jax.experimental.pallas import tpu as pltpu

F32 = jnp.float32
BF16 = jnp.bfloat16

D_MODEL = 1024
N_SSM_GROUPS = 64
SSM_GROUP = 16
SSM_STATE = 64
SSM_CHUNK = 16
N_ATT_GROUPS = 3
WINDOWS = (128, 512, 2048)
DILATIONS = (1, 4, 16)
HEAD_DIM = 64
N_HEADS = 16
ATTN_WIDTH = N_HEADS * HEAD_DIM
ROT_DIM = HEAD_DIM // 4
ROPE_THETA = 500000.0
WINDOW_KEYS = 128
LN_EPS = 1e-5
NEG_INF = -1e30
LANES = 128
SUBLANES = 8
VMEM_LIMIT = 52 * 1024 * 1024


def _sigmoid(x):
    return 1.0 / (1.0 + jnp.exp(-x))


def _gelu_tanh(x):
    return 0.5 * x * (1.0 + jnp.tanh(math.sqrt(2.0 / math.pi) * (x + 0.044715 * (x * x * x))))


def _mm_kernel(x_ref, w_ref, o_ref):
    o_ref[...] = jnp.dot(x_ref[...], w_ref[...], preferred_element_type=F32).astype(o_ref.dtype)


def _mm_rope_kernel(x_ref, w_ref, cos_ref, sa_ref, sb_ref, o_ref, *, rot_blocks):
    j = pl.program_id(1)
    acc = jnp.dot(x_ref[...], w_ref[...], preferred_element_type=F32)

    @pl.when(j < rot_blocks)
    def _():
        o_ref[...] = _rope(acc, cos_ref, sa_ref, sb_ref).astype(o_ref.dtype)

    @pl.when(j >= rot_blocks)
    def _():
        o_ref[...] = acc.astype(o_ref.dtype)


def _token_tile(n_tokens, want):
    tm = min(want, n_tokens)
    assert n_tokens % tm == 0
    return tm


def _project(x, w, *, rope=None, rot_cols=0, tn=512, out_dtype=BF16):
    t, k = x.shape
    n = w.shape[1]
    tm = _token_tile(t, 1024)
    grid = (t // tm, n // tn)
    x_spec = pl.BlockSpec((tm, k), lambda i, j: (i, 0))
    w_spec = pl.BlockSpec((k, tn), lambda i, j: (0, j))
    o_spec = pl.BlockSpec((tm, tn), lambda i, j: (i, j))
    params = pltpu.CompilerParams(dimension_semantics=("parallel", "arbitrary"), vmem_limit_bytes=VMEM_LIMIT)
    if rope is None:
        return pl.pallas_call(
            _mm_kernel, grid=grid, in_specs=[x_spec, w_spec], out_specs=o_spec,
            out_shape=jax.ShapeDtypeStruct((t, n), out_dtype), compiler_params=params, name="proj",
        )(x, w)
    assert rot_cols % tn == 0
    tab_spec = pl.BlockSpec((tm, LANES), lambda i, j: (i, 0))
    return pl.pallas_call(
        functools.partial(_mm_rope_kernel, rot_blocks=rot_cols // tn), grid=grid,
        in_specs=[x_spec, w_spec, tab_spec, tab_spec, tab_spec], out_specs=o_spec,
        out_shape=jax.ShapeDtypeStruct((t, n), out_dtype), compiler_params=params, name="proj_rope",
    )(x, w, *rope)


def _rope(acc, cos_ref, sa_ref, sb_ref):
    tn = acc.shape[1]
    reps = tn // LANES
    cos = jnp.concatenate([cos_ref[...]] * reps, axis=1)
    sa = jnp.concatenate([sa_ref[...]] * reps, axis=1)
    sb = jnp.concatenate([sb_ref[...]] * reps, axis=1)
    half = ROT_DIM // 2
    return acc * cos + pltpu.roll(acc, tn - half, 1) * sa + pltpu.roll(acc, half, 1) * sb


def _mm_deint_kernel(*refs, dil, rope):
    if rope:
        x_ref, w_ref, cos_ref, sa_ref, sb_ref, o_ref = refs[:6]
    else:
        x_ref, w_ref, o_ref = refs[:3]
    acc = jnp.dot(x_ref[...], w_ref[...], preferred_element_type=F32)
    if rope:
        acc = _rope(acc, cos_ref, sa_ref, sb_ref)
    if dil == 1:
        o_ref[0] = acc.astype(o_ref.dtype)
    else:
        acc_scr = refs[-1]
        rows = acc.shape[0] // dil
        for c in range(acc.shape[1] // LANES):
            sl = slice(c * LANES, (c + 1) * LANES)
            acc_scr[c] = acc[:, sl]
            for r in range(dil):
                o_ref[r, :, sl] = acc_scr[c, pl.ds(r, rows, stride=dil), :].astype(o_ref.dtype)


def _project_deint(x, w, col, dil, rope=None):
    t, k = x.shape
    tm = _token_tile(t, 512)
    wn = D_MODEL
    in_specs = [pl.BlockSpec((tm, k), lambda i: (i, 0)), pl.BlockSpec((k, wn), lambda i: (0, col))]
    args = [x, w]
    if rope is not None:
        in_specs += [pl.BlockSpec((tm, LANES), lambda i: (i, 0))] * 3
        args += list(rope)
    return pl.pallas_call(
        functools.partial(_mm_deint_kernel, dil=dil, rope=rope is not None), grid=(t // tm,),
        in_specs=in_specs, out_specs=pl.BlockSpec((dil, tm // dil, wn), lambda i: (0, i, 0)),
        out_shape=jax.ShapeDtypeStruct((dil, t // dil, wn), BF16),
        scratch_shapes=[pltpu.VMEM((wn // LANES, tm, LANES), F32)] if dil > 1 else [],
        compiler_params=pltpu.CompilerParams(dimension_semantics=("parallel",), vmem_limit_bytes=VMEM_LIMIT),
        name=f"proj_deint{dil}",
    )(*args)


def _rope_tables(pos, scale):
    half = ROT_DIM // 2
    inv_freq = ROPE_THETA ** (-jnp.arange(0, ROT_DIM, 2, dtype=F32) / ROT_DIM)
    ang = pos[:, None] * inv_freq[None, :]
    cos, sin = jnp.cos(ang), jnp.sin(ang)
    t = pos.shape[0]
    ones = jnp.ones((t, HEAD_DIM - ROT_DIM), F32)
    zeros = jnp.zeros((t, HEAD_DIM - ROT_DIM), F32)
    zhalf = jnp.zeros((t, half), F32)
    cos64 = jnp.concatenate([cos, cos, ones], axis=1)
    sa64 = jnp.concatenate([-sin, zhalf, zeros], axis=1)
    sb64 = jnp.concatenate([zhalf, sin, zeros], axis=1)
    return tuple(scale * jnp.concatenate([a, a], axis=1) for a in (cos64, sa64, sb64))


def _ssm_operators(a_re, a_im, log_dt, b_re, b_im, c_re, c_im, d_skip):
    g, p, m, c = N_SSM_GROUPS, SSM_STATE, SSM_GROUP, SSM_CHUNK
    hi = lax.Precision.HIGHEST
    a = lax.complex(a_re.astype(F32), a_im.astype(F32))
    dta = jnp.exp(log_dt.astype(F32))[:, None] * a
    a_bar = jnp.exp(dta)
    b_bar = ((a_bar - 1.0) / a)[..., None] * lax.complex(b_re.astype(F32), b_im.astype(F32))
    cc = lax.complex(c_re.astype(F32), c_im.astype(F32))
    pw = jnp.exp(jnp.arange(c + 1, dtype=F32)[:, None, None] * dta[None])

    cp = cc[None] * pw[:c, :, None, :]
    kern = (jnp.einsum("tgmp,gpn->tgmn", cp.real, b_bar.real, precision=hi)
            - jnp.einsum("tgmp,gpn->tgmn", cp.imag, b_bar.imag, precision=hi))
    s_idx = jnp.arange(c)[:, None]
    t_idx = jnp.arange(c)[None, :]
    lag = t_idx - s_idx
    blocks = kern[jnp.clip(lag, 0, c - 1)]
    blocks = jnp.where((lag >= 0)[:, :, None, None, None], blocks, 0.0)
    t_op = blocks.transpose(2, 0, 4, 1, 3).reshape(g, c * m, c * m)
    d_rep = jnp.tile(d_skip.astype(F32).reshape(g, m), (1, c))
    t_op = t_op + jnp.eye(c * m, dtype=F32)[None] * d_rep[:, None, :]

    bin_c = pw[c - 1 - jnp.arange(c)][:, :, :, None] * b_bar[None]
    bin_c = bin_c.transpose(1, 0, 3, 2).reshape(g, c * m, p)
    b_in = jnp.concatenate([bin_c.real, bin_c.imag], axis=-1)

    w = cc[None] * pw[1:, :, None, :]
    w = w.transpose(1, 3, 0, 2).reshape(g, p, c * m)
    c_out = jnp.concatenate([w.real, -w.imag], axis=1)

    steps = jnp.array([16, 32, 64, 128, 8, 0, 0, 0] + [16 * r for r in range(SUBLANES)], F32)
    pows = jnp.exp(steps[None, :, None] * dta[:, None, :])
    ca = jnp.concatenate([pows.real, pows.real], axis=-1)
    cb = jnp.concatenate([-pows.imag, pows.imag], axis=-1)
    return t_op.astype(BF16), b_in.astype(BF16), c_out.astype(BF16), ca, cb


def _swap_halves(x):
    return pltpu.roll(x, SSM_STATE, x.ndim - 1)


def _cmul(x, xs, ca, cb):
    return x * ca + xs * cb


def _ssm_kernel(x_ref, xs_ref, h0_ref, t_ref, bin_ref, cout_ref, ca_ref, cb_ref,
                y_ref, hl_ref, ys_ref, hs_ref, l_scr, ls_scr, e_scr, *, n_seq, rows_per_seq):
    rows = n_seq * rows_per_seq
    n_blocks = rows_per_seq // SUBLANES
    ca = ca_ref[0]
    cb = cb_ref[0]
    x = x_ref[0]
    v = jnp.dot(x, bin_ref[0], preferred_element_type=F32)
    rmod = lax.broadcasted_iota(jnp.int32, (rows, LANES), 0) & (SUBLANES - 1)

    for k, dist in enumerate((1, 2, 4)):
        sh = jnp.where(rmod >= dist, pltpu.roll(v, dist, 0), 0.0)
        v = v + _cmul(sh, _swap_halves(sh), ca[k:k + 1], cb[k:k + 1])
    l_scr[...] = v
    ls_scr[...] = _swap_halves(v)

    ca128 = ca[3:4]
    cb128 = cb[3:4]
    cb128s = _swap_halves(cb128)

    def body(b, carry):
        e, es = carry
        lasts, lasts_s = [], []
        for s in range(n_seq):
            base = s * rows_per_seq + b * SUBLANES
            e_scr[pl.ds(base, SUBLANES), :] = jnp.broadcast_to(e[s:s + 1], (SUBLANES, LANES))
            lasts.append(l_scr[pl.ds(base + SUBLANES - 1, 1), :])
            lasts_s.append(ls_scr[pl.ds(base + SUBLANES - 1, 1), :])
        e_new = jnp.concatenate(lasts, axis=0) + e * ca128 + es * cb128
        es_new = jnp.concatenate(lasts_s, axis=0) + es * ca128 + e * cb128s
        return e_new, es_new

    zero = jnp.zeros((n_seq, LANES), F32)
    e_fin, _ = lax.fori_loop(0, n_blocks, body, (zero, zero))
    hl_ref[0] = e_fin

    lshift = jnp.where(rmod >= 1, pltpu.roll(v, 1, 0), 0.0)
    ebc = e_scr[...]
    pwa = ca[SUBLANES:2 * SUBLANES]
    pwb = cb[SUBLANES:2 * SUBLANES]
    shape3 = (rows // SUBLANES, SUBLANES, LANES)
    h = (lshift.reshape(shape3) + ebc.reshape(shape3) * pwa[None]
         + _swap_halves(ebc).reshape(shape3) * pwb[None]).reshape(rows, LANES)
    y = jnp.dot(x, t_ref[0], preferred_element_type=F32)
    y = y + jnp.dot(h.astype(BF16), cout_ref[0], preferred_element_type=F32)
    y_ref[0] = y.astype(y_ref.dtype)

    half = SSM_CHUNK * SSM_GROUP // 2
    xs = xs_ref[0]
    h0 = h0_ref[0]
    ys = jnp.dot(xs, t_ref[0, :half, :half], preferred_element_type=F32)
    ys = ys + jnp.dot(h0.astype(BF16), cout_ref[0, :, :half], preferred_element_type=F32)
    ys_ref[0] = ys.astype(ys_ref.dtype)
    hs = _cmul(h0, _swap_halves(h0), ca[4:5], cb[4:5])
    hs_ref[0] = hs + jnp.dot(xs, bin_ref[0, half:, :], preferred_element_type=F32)


def _ssm(xg, xsg, h0g, ops, *, n_seq):
    t_op, b_in, c_out, ca, cb = ops
    g, rows, cw = xg.shape
    n_s = xsg.shape[1]
    hw = xsg.shape[2]

    def spec(shape):
        return pl.BlockSpec((1,) + shape, lambda i: (i, 0, 0))

    return pl.pallas_call(
        functools.partial(_ssm_kernel, n_seq=n_seq, rows_per_seq=rows // n_seq),
        grid=(g,),
        in_specs=[spec((rows, cw)), spec((n_s, hw)), spec((n_s, LANES)), spec((cw, cw)), spec((cw, LANES)),
                  spec((LANES, cw)), spec((2 * SUBLANES, LANES)), spec((2 * SUBLANES, LANES))],
        out_specs=[spec((rows, cw)), spec((n_seq, LANES)), spec((n_s, hw)), spec((n_s, LANES))],
        out_shape=[jax.ShapeDtypeStruct((g, rows, cw), BF16), jax.ShapeDtypeStruct((g, n_seq, LANES), F32),
                   jax.ShapeDtypeStruct((g, n_s, hw), BF16), jax.ShapeDtypeStruct((g, n_s, LANES), F32)],
        scratch_shapes=[pltpu.VMEM((rows, LANES), F32)] * 3,
        compiler_params=pltpu.CompilerParams(dimension_semantics=("parallel",), vmem_limit_bytes=VMEM_LIMIT),
        name="s5_scan",
    )(xg, xsg, h0g, t_op, b_in, c_out, ca, cb)


def _post_layer(x, sub, p, g_ref, b_ref, wpg_ref, wpe_ref, alpha):
    r = alpha * x + sub
    mu = jnp.mean(r, axis=-1, keepdims=True)
    cen = r - mu
    var = jnp.mean(cen * cen, axis=-1, keepdims=True)
    h = cen * lax.rsqrt(var + LN_EPS) * g_ref[...] + b_ref[...]
    gate = _sigmoid(jnp.dot(h.astype(BF16), wpg_ref[...], preferred_element_type=F32))
    ple = jnp.dot(p.astype(BF16), wpe_ref[...], preferred_element_type=F32)
    return h + gate * ple


def _glu_post_kernel(y_ref, z_ref, x_ref, p_ref, g_ref, b_ref, wglu_ref, wout_ref, wpg_ref, wpe_ref,
                     xo_ref, xb_ref, *, alpha):
    g = _gelu_tanh(y_ref[...].astype(F32))
    t = jnp.dot(g.astype(BF16), wglu_ref[...], preferred_element_type=F32)
    z = z_ref[...].astype(F32)
    glu = g * _sigmoid(t) * (z * _sigmoid(z))
    sub = jnp.dot(glu.astype(BF16), wout_ref[...], preferred_element_type=F32)
    out = _post_layer(x_ref[...], sub, p_ref[...], g_ref, b_ref, wpg_ref, wpe_ref, alpha)
    xo_ref[...] = out
    xb_ref[...] = out.astype(BF16)


def _interleave_rows(ref, scr, dil):
    if dil == 1:
        return ref[0].astype(F32)
    rows = ref.shape[1]
    slabs = []
    for c in range(ref.shape[2] // LANES):
        for r in range(dil):
            scr[c, pl.ds(r, rows, stride=dil), :] = ref[r, :, c * LANES:(c + 1) * LANES].astype(F32)
        slabs.append(scr[c])
    return slabs[0] if len(slabs) == 1 else jnp.concatenate(slabs, axis=1)


def _comb_post_kernel(*refs, dils, alpha):
    n = len(dils)
    if n == 1:
        o_ref = refs[0]
        z_ref, x_ref, p_ref, g_ref, b_ref, wout_ref, wpg_ref, wpe_ref, xo_ref, xb_ref = refs[1:]
        o = o_ref[...].astype(F32)
    else:
        o_refs, lse_refs = refs[:n], refs[n:2 * n]
        z_ref, x_ref, p_ref, g_ref, b_ref, wout_ref, wpg_ref, wpe_ref, xo_ref, xb_ref = refs[2 * n:2 * n + 10]
        scrs = list(refs[2 * n + 10:])
        lses = [_interleave_rows(r, scrs.pop(0) if d > 1 else None, d) for r, d in zip(lse_refs, dils)]
        top = functools.reduce(jnp.maximum, lses)
        ws = [jnp.exp(l - top) for l in lses]
        tot = functools.reduce(lambda a, c: a + c, ws)
        src = lax.broadcasted_iota(jnp.int32, (LANES, ATTN_WIDTH), 0)
        dst = lax.broadcasted_iota(jnp.int32, (LANES, ATTN_WIDTH), 1)
        lanes_per_head = LANES // N_HEADS
        expand = jnp.where(src == (dst >> (HEAD_DIM.bit_length() - 1)) * lanes_per_head, 1.0, 0.0).astype(BF16)
        o = None
        for w, r, d in zip(ws, o_refs, dils):
            a = w / tot
            hi = a.astype(BF16)
            lo = (a - hi.astype(F32)).astype(BF16)
            wide = (jnp.dot(hi, expand, preferred_element_type=F32) + jnp.dot(lo, expand, preferred_element_type=F32))
            term = wide * _interleave_rows(r, scrs.pop(0) if d > 1 else None, d)
            o = term if o is None else o + term
    z = z_ref[...].astype(F32)
    gated = o * (z * _sigmoid(z))
    sub = jnp.dot(gated.astype(BF16), wout_ref[...], preferred_element_type=F32)
    out = _post_layer(x_ref[...], sub, p_ref[...], g_ref, b_ref, wpg_ref, wpe_ref, alpha)
    xo_ref[...] = out
    xb_ref[...] = out.astype(BF16)


def _tok_spec(tm, width, col=0):
    return pl.BlockSpec((tm, width), lambda i: (i, col))


def _full_spec(shape):
    return pl.BlockSpec(shape, lambda i: (0,) * len(shape))


def _post_outputs(t, tm):
    return dict(
        out_specs=[_tok_spec(tm, D_MODEL), _tok_spec(tm, D_MODEL)],
        out_shape=[jax.ShapeDtypeStruct((t, D_MODEL), F32), jax.ShapeDtypeStruct((t, D_MODEL), BF16)],
        compiler_params=pltpu.CompilerParams(dimension_semantics=("parallel",), vmem_limit_bytes=VMEM_LIMIT),
    )


def _glu_post(y, uz, x, p, ln_g, ln_b, w_glu, w_out, w_pg, w_pe, alpha):
    t = x.shape[0]
    tm = _token_tile(t, 512)
    ple = p.shape[1]
    return pl.pallas_call(
        functools.partial(_glu_post_kernel, alpha=alpha), grid=(t // tm,),
        in_specs=[_tok_spec(tm, D_MODEL), _tok_spec(tm, D_MODEL, 1), _tok_spec(tm, D_MODEL), _tok_spec(tm, ple),
                  _full_spec((1, D_MODEL)), _full_spec((1, D_MODEL)), _full_spec((D_MODEL, D_MODEL)),
                  _full_spec((D_MODEL, D_MODEL)), _full_spec((D_MODEL, D_MODEL)), _full_spec((ple, D_MODEL))],
        name="glu_post", **_post_outputs(t, tm),
    )(y, uz, x, p, ln_g, ln_b, w_glu, w_out, w_pg, w_pe)


def _comb_post(os_, lses, dils, z, z_col, x, p, ln_g, ln_b, w_out, w_pg, w_pe, alpha):
    t = x.shape[0]
    tm = _token_tile(t, 512)
    ple = p.shape[1]
    if len(dils) == 1:
        group_specs = [_tok_spec(tm, D_MODEL)]
        scratch = []
    else:
        group_specs = [pl.BlockSpec((d, tm // d, D_MODEL), lambda i: (0, i, 0)) for d in dils]
        group_specs += [pl.BlockSpec((d, tm // d, LANES), lambda i: (0, i, 0)) for d in dils]
        scratch = [pltpu.VMEM((1, tm, LANES), F32) for d in dils if d > 1]
        scratch += [pltpu.VMEM((D_MODEL // LANES, tm, LANES), F32) for d in dils if d > 1]
    return pl.pallas_call(
        functools.partial(_comb_post_kernel, dils=tuple(dils), alpha=alpha), grid=(t // tm,),
        in_specs=group_specs
        + [_tok_spec(tm, D_MODEL, z_col), _tok_spec(tm, D_MODEL), _tok_spec(tm, ple),
           _full_spec((1, D_MODEL)), _full_spec((1, D_MODEL)), _full_spec((D_MODEL, D_MODEL)),
           _full_spec((D_MODEL, D_MODEL)), _full_spec((ple, D_MODEL))],
        scratch_shapes=scratch, name="attn_out_post", **_post_outputs(t, tm),
    )(*os_, *lses, z, x, p, ln_g, ln_b, w_out, w_pg, w_pe)


def _attn_prompt_kernel(q_ref, kp_ref, kc_ref, vp_ref, vc_ref, o_ref, lse_ref):
    ib = pl.program_id(2)
    tq = q_ref.shape[0]
    row = lax.broadcasted_iota(jnp.int32, (tq, 2 * tq), 0)
    col = lax.broadcasted_iota(jnp.int32, (tq, 2 * tq), 1)
    back = row + tq - col
    valid = (back >= 0) & (back <= WINDOW_KEYS) & ((col >= tq) | (ib > 0))
    lane = lax.broadcasted_iota(jnp.int32, (tq, LANES), 1)
    low = lane < HEAD_DIM
    head_of_lane = lane >> ((LANES // N_HEADS).bit_length() - 1)
    lse_tile = jnp.zeros((tq, LANES), F32)
    nt = (((1,), (1,)), ((), ()))
    for j in range(ATTN_WIDTH // LANES):
        sl = slice(j * LANES, (j + 1) * LANES)
        q2 = q_ref[:, sl]
        k2 = jnp.concatenate([kp_ref[:, sl], kc_ref[:, sl]], axis=0)
        v2 = jnp.concatenate([vp_ref[:, sl], vc_ref[:, sl]], axis=0)
        outs = []
        for sel in (low, jnp.logical_not(low)):
            qm = jnp.where(sel, q2, jnp.zeros_like(q2))
            s = lax.dot_general(qm, k2, nt, preferred_element_type=F32)
            s = jnp.where(valid, s, NEG_INF)
            m = jnp.max(s, axis=1, keepdims=True)
            pr = jnp.exp(s - m)
            l = jnp.sum(pr, axis=1, keepdims=True)
            o = jnp.dot(pr.astype(BF16), v2, preferred_element_type=F32)
            outs.append((o / l, m + jnp.log(l)))
        o_ref[:, sl] = jnp.where(low, outs[0][0], outs[1][0]).astype(o_ref.dtype)
        for hh in range(2):
            lse_tile = jnp.where(head_of_lane == 2 * j + hh, outs[hh][1], lse_tile)
    lse_ref[...] = lse_tile


def _attn_prompt_group(q, k, v, n_seq, seq_len):
    dil, rows, _ = q.shape
    tq = WINDOW_KEYS
    nb = seq_len // dil // tq
    assert rows == n_seq * nb * tq

    def cur(b, r, ib):
        return (r, b * nb + ib, 0)

    def prev(b, r, ib):
        return (r, b * nb + jnp.maximum(ib - 1, 0), 0)

    blk = (None, tq, D_MODEL)
    return pl.pallas_call(
        _attn_prompt_kernel, grid=(n_seq, dil, nb),
        in_specs=[pl.BlockSpec(blk, cur), pl.BlockSpec(blk, prev), pl.BlockSpec(blk, cur),
                  pl.BlockSpec(blk, prev), pl.BlockSpec(blk, cur)],
        out_specs=[pl.BlockSpec(blk, cur), pl.BlockSpec((None, tq, LANES), cur)],
        out_shape=[jax.ShapeDtypeStruct((dil, rows, D_MODEL), BF16), jax.ShapeDtypeStruct((dil, rows, LANES), F32)],
        compiler_params=pltpu.CompilerParams(dimension_semantics=("parallel", "parallel", "arbitrary"),
                                             vmem_limit_bytes=VMEM_LIMIT),
        name=f"attn_prompt_d{dil}",
    )(q, k, k, v, v)


SAMPLE_KEY_CHUNK = 512


def _attn_sample_kernel(q_ref, kvn_ref, c0_ref, c1_ref, c2_ref, o_ref, s_scr, m_scr):
    step = pl.program_id(1)
    ds = q_ref.shape[1]
    n_rows = N_HEADS * ds
    assert ds & (ds - 1) == 0 and n_rows == LANES
    rho = lax.broadcasted_iota(jnp.int32, (n_rows, ATTN_WIDTH), 0)
    lane = lax.broadcasted_iota(jnp.int32, (n_rows, ATTN_WIDTH), 1)
    head_mask = (rho >> (ds.bit_length() - 1)) == (lane >> (HEAD_DIM.bit_length() - 1))
    cache_refs = (c0_ref, c1_ref, c2_ref)
    nt = (((1,), (1,)), ((), ()))
    kvn = kvn_ref[0].astype(F32)

    def new_block(g, off):
        piece = kvn[:, off + g * ATTN_WIDTH: off + (g + 1) * ATTN_WIDTH]
        pad = jnp.zeros((LANES - ds, ATTN_WIDTH), F32)
        return jnp.concatenate([piece, pad], axis=0).astype(BF16)

    chunks = []
    col = 0
    for g, c_ref in enumerate(cache_refs):
        width = c_ref.shape[3]
        for c0 in range(0, width, SAMPLE_KEY_CHUNK):
            cw = min(SAMPLE_KEY_CHUNK, width - c0)
            chunks.append((g, c0, cw, col))
            col += cw
    new_cols = [col + g * LANES for g in range(N_ATT_GROUPS)]

    @pl.when(step == 0)
    def _():
        q_all = q_ref[0].astype(F32)
        q_rows = []
        for g in range(N_ATT_GROUPS):
            qg = q_all[:, g * ATTN_WIDTH:(g + 1) * ATTN_WIDTH]
            q_rows.append(jnp.where(head_mask, jnp.concatenate([qg] * N_HEADS, axis=0), 0.0).astype(BF16))
        m = jnp.full((n_rows, 1), NEG_INF, F32)
        for g, c0, cw, col0 in chunks:
            kt = cache_refs[g][0, 0, :, c0:c0 + cw].astype(BF16)
            s = jnp.dot(q_rows[g], kt, preferred_element_type=F32)
            qi = lax.broadcasted_iota(jnp.int32, (n_rows, cw), 0) & (ds - 1)
            back = lax.broadcasted_iota(jnp.int32, (n_rows, cw), 1) + c0 - qi
            s = jnp.where((back >= 0) & ((back & (DILATIONS[g] - 1)) == 0), s, NEG_INF)
            s_scr[:, col0:col0 + cw] = s
            m = jnp.maximum(m, jnp.max(s, axis=1, keepdims=True))
        qi = lax.broadcasted_iota(jnp.int32, (n_rows, LANES), 0) & (ds - 1)
        kappa = lax.broadcasted_iota(jnp.int32, (n_rows, LANES), 1)
        for g in range(N_ATT_GROUPS):
            s = lax.dot_general(q_rows[g], new_block(g, 0), nt, preferred_element_type=F32)
            ahead = qi - kappa
            s = jnp.where((ahead >= 0) & ((ahead & (DILATIONS[g] - 1)) == 0) & (kappa < ds), s, NEG_INF)
            s_scr[:, new_cols[g]:new_cols[g] + LANES] = s
            m = jnp.maximum(m, jnp.max(s, axis=1, keepdims=True))
        m_scr[...] = jnp.broadcast_to(m, m_scr.shape)

    @pl.when(step == 1)
    def _():
        m = m_scr[:, 0:1]
        l = jnp.zeros((n_rows, 1), F32)
        acc = jnp.zeros((n_rows, ATTN_WIDTH), F32)
        for g, c0, cw, col0 in chunks:
            pr = jnp.exp(s_scr[:, col0:col0 + cw] - m)
            l = l + jnp.sum(pr, axis=1, keepdims=True)
            vt = cache_refs[g][0, 0, :, c0:c0 + cw].astype(BF16)
            acc = acc + lax.dot_general(pr.astype(BF16), vt, nt, preferred_element_type=F32)
        for g in range(N_ATT_GROUPS):
            pr = jnp.exp(s_scr[:, new_cols[g]:new_cols[g] + LANES] - m)
            l = l + jnp.sum(pr, axis=1, keepdims=True)
            acc = acc + jnp.dot(pr.astype(BF16), new_block(g, N_ATT_GROUPS * ATTN_WIDTH),
                                preferred_element_type=F32)
        acc = acc / l
        acc3 = acc.reshape(N_HEADS, ds, ATTN_WIDTH)
        low = lax.broadcasted_iota(jnp.int32, (ds, LANES), 1) < HEAD_DIM
        for c in range(ATTN_WIDTH // LANES):
            sl = slice(c * LANES, (c + 1) * LANES)
            o_ref[0, :, sl] = jnp.where(low, acc3[2 * c][:, sl], acc3[2 * c + 1][:, sl])


def _cache_views(caches):
    views = []
    for cache, dil in zip(caches, DILATIONS):
        n, width = cache.shape[:2]
        assert width == WINDOW_KEYS * dil and cache.shape[2:] == (2, N_HEADS, HEAD_DIM)
        views.append(cache.transpose(0, 2, 3, 4, 1).reshape(n, 2, ATTN_WIDTH, width))
    return views


def _attn_sample(qz_s, kv_s, views, n_seq, ds):
    q3 = qz_s.reshape(n_seq, ds, qz_s.shape[1])
    kv3 = kv_s.reshape(n_seq, ds, kv_s.shape[1])
    n_cols = sum(v.shape[3] for v in views) + N_ATT_GROUPS * LANES

    def tok_spec(width):
        return pl.BlockSpec((1, ds, width), lambda n, s: (n, 0, 0))

    o = pl.pallas_call(
        _attn_sample_kernel, grid=(n_seq, 2),
        in_specs=[tok_spec(q3.shape[2]), tok_spec(kv3.shape[2])]
        + [pl.BlockSpec((1, 1, ATTN_WIDTH, v.shape[3]), lambda n, s: (n, s, 0, 0)) for v in views],
        out_specs=tok_spec(ATTN_WIDTH),
        out_shape=jax.ShapeDtypeStruct((n_seq, ds, ATTN_WIDTH), F32),
        scratch_shapes=[pltpu.VMEM((N_HEADS * ds, n_cols), F32), pltpu.VMEM((N_HEADS * ds, LANES), F32)],
        compiler_params=pltpu.CompilerParams(dimension_semantics=("parallel", "arbitrary"),
                                             vmem_limit_bytes=VMEM_LIMIT),
        name="attn_sample",
    )(q3, kv3, *views)
    return o.reshape(n_seq * ds, ATTN_WIDTH)


def kernel(x_prompt, x_sample, state_ssm_re, state_ssm_im, cache_kv_w128, cache_kv_w512, cache_kv_w2048, p_prompt, p_sample, ln_g, ln_b, w_pe, w_pg, w_in_a, a_re, a_im, log_dt, b_re, b_im, c_re, c_im, d_skip, w_glu, w_out_a, w_kv, w_in_b, w_out_b):
    depth = ln_g.shape[0]
    n_a = w_in_a.shape[0]
    batch, seq, _ = x_prompt.shape
    n_s, ds, _ = x_sample.shape
    past = cache_kv_w2048.shape[1]
    assert ds * 2 == SSM_CHUNK and ds <= DILATIONS[2] and past == WINDOWS[2] and seq % (DILATIONS[2] * WINDOW_KEYS) == 0
    alpha = (2 * depth) ** 0.25
    t_p, t_s = batch * seq, n_s * ds
    g, cw = N_SSM_GROUPS, SSM_CHUNK * SSM_GROUP
    caches = (cache_kv_w128, cache_kv_w512, cache_kv_w2048)

    xs = [x_prompt.reshape(t_p, D_MODEL), x_sample.reshape(t_s, D_MODEL)]
    xbs = [v.astype(BF16) for v in xs]
    ps = [p_prompt.reshape(depth, t_p, -1), p_sample.reshape(depth, t_s, -1)]
    wb = lambda w: w.astype(BF16)
    h_prompt, h_sample = [], []

    for i in range(n_a):
        uzs = [_project(xb, wb(w_in_a[i]), tn=512) for xb in xbs]
        ops = _ssm_operators(a_re[i], a_im[i], log_dt[i], b_re[i], b_im[i], c_re[i], c_im[i], d_skip[i])
        xg = uzs[0][:, :D_MODEL].reshape(t_p // SSM_CHUNK, SSM_CHUNK, g, SSM_GROUP)
        xg = xg.transpose(2, 0, 1, 3).reshape(g, t_p // SSM_CHUNK, cw)
        xsg = uzs[1][:, :D_MODEL].reshape(n_s, ds, g, SSM_GROUP).transpose(2, 0, 1, 3).reshape(g, n_s, cw // 2)
        h0g = jnp.concatenate([state_ssm_re[i], state_ssm_im[i]], axis=-1).astype(F32).transpose(1, 0, 2)
        yg, hl, ysg, hs = _ssm(xg, xsg, h0g, ops, n_seq=batch)
        y_p = yg.reshape(g, t_p // SSM_CHUNK, SSM_CHUNK, SSM_GROUP).transpose(1, 2, 0, 3).reshape(t_p, D_MODEL)
        y_s = ysg.reshape(g, n_s, ds, SSM_GROUP).transpose(1, 2, 0, 3).reshape(t_s, D_MODEL)
        h_prompt.append(hl.transpose(1, 0, 2))
        h_sample.append(hs.transpose(1, 0, 2))
        for k, y in enumerate((y_p, y_s)):
            xs[k], xbs[k] = _glu_post(y, uzs[k], xs[k], ps[k][i], ln_g[i:i + 1], ln_b[i:i + 1], wb(w_glu[i]),
                                      wb(w_out_a[i]), wb(w_pg[i]), wb(w_pe[i]), alpha)

    pos = [jnp.tile(jnp.arange(seq, dtype=F32), batch), jnp.tile(past + jnp.arange(ds, dtype=F32), n_s)]
    rope_k = [_rope_tables(p_, 1.0) for p_ in pos]
    rope_q = [_rope_tables(p_, HEAD_DIM ** -0.5) for p_ in pos]
    q_width = N_ATT_GROUPS * ATTN_WIDTH
    n_grp = N_ATT_GROUPS
    w_kv_b = wb(w_kv)
    k_p = [_project_deint(xbs[0], w_kv_b, grp, DILATIONS[grp], rope_k[0]) for grp in range(n_grp)]
    v_p = [_project_deint(xbs[0], w_kv_b, n_grp + grp, DILATIONS[grp]) for grp in range(n_grp)]
    kv_s = _project(xbs[1], w_kv_b, rope=rope_k[1], rot_cols=q_width, tn=512)
    views = _cache_views(caches)

    for i in range(n_a, depth):
        j = i - n_a
        w_in = wb(w_in_b[j])
        post_w = (ln_g[i:i + 1], ln_b[i:i + 1], wb(w_out_b[j]), wb(w_pg[i]), wb(w_pe[i]), alpha)
        q_p = [_project_deint(xbs[0], w_in, grp, DILATIONS[grp], rope_q[0]) for grp in range(n_grp)]
        z_p = _project_deint(xbs[0], w_in, n_grp, 1).reshape(t_p, D_MODEL)
        parts = [_attn_prompt_group(q_p[grp], k_p[grp], v_p[grp], batch, seq) for grp in range(n_grp)]
        qz_s = _project(xbs[1], w_in, rope=rope_q[1], rot_cols=q_width, tn=512)
        o_s = _attn_sample(qz_s, kv_s, views, n_s, ds)
        xs[0], xbs[0] = _comb_post([pt[0] for pt in parts], [pt[1] for pt in parts], DILATIONS, z_p, 0,
                                   xs[0], ps[0][i], *post_w)
        xs[1], xbs[1] = _comb_post([o_s], [], (1,), qz_s, n_grp, xs[1], ps[1][i], *post_w)

    y_prompt = xs[0].reshape(batch, seq, D_MODEL)
    y_sample = xs[1].reshape(n_s, ds, D_MODEL)
    hp = jnp.stack(h_prompt)
    hsm = jnp.stack(h_sample)

    def prompt_window(arr, dil, width):
        per = arr.reshape(dil, batch, seq // dil, N_HEADS, HEAD_DIM)[:, :, (seq - width) // dil:]
        return per.transpose(1, 2, 0, 3, 4).reshape(batch, width, N_HEADS, HEAD_DIM).astype(F32)

    kv_prompt = [jnp.stack([prompt_window(k_p[grp], DILATIONS[grp], min(w, seq)),
                            prompt_window(v_p[grp], DILATIONS[grp], min(w, seq))], axis=2)
                 for grp, w in enumerate(WINDOWS)]
    kv_s6 = kv_s.astype(F32).reshape(n_s, ds, 2, N_ATT_GROUPS, N_HEADS, HEAD_DIM)
    kv_sample = [kv_s6[:, :, :, grp] for grp in range(N_ATT_GROUPS)]
    return (y_prompt, y_sample, hp[..., :SSM_STATE], hp[..., SSM_STATE:], hsm[..., :SSM_STATE], hsm[..., SSM_STATE:],
            *kv_prompt, *kv_sample)
```

```python
import functools
import math

import jax
import jax.numpy as jnp
from jax import lax
from jax.experimental import pallas as pl
from jax.experimental.pallas import tpu as pltpu

F32 = jnp.float32
BF16 = jnp.bfloat16

D_MODEL = 1024
N_SSM_GROUPS = 64
SSM_GROUP = 16
SSM_STATE = 64
SSM_CHUNK = 16
N_ATT_GROUPS = 3
WINDOWS = (128, 512, 2048)
DILATIONS = (1, 4, 16)
HEAD_DIM = 64
N_HEADS = 16
ATTN_WIDTH = N_HEADS * HEAD_DIM
ROT_DIM = HEAD_DIM // 4
ROPE_THETA = 500000.0
WINDOW_KEYS = 128
LN_EPS = 1e-5
NEG_INF = -1e30
LANES = 128
SUBLANES = 8
VMEM_LIMIT = 52 * 1024 * 1024


def _sigmoid(x):
    return 1.0 / (1.0 + jnp.exp(-x))


def _gelu_tanh(x):
    return 0.5 * x * (1.0 + jnp.tanh(math.sqrt(2.0 / math.pi) * (x + 0.044715 * (x * x * x))))


def _mm_kernel(x_ref, w_ref, o_ref):
    o_ref[...] = jnp.dot(x_ref[...], w_ref[...], preferred_element_type=F32).astype(o_ref.dtype)


def _mm_rope_kernel(x_ref, w_ref, cos_ref, sa_ref, sb_ref, o_ref, *, rot_blocks):
    j = pl.program_id(1)
    acc = jnp.dot(x_ref[...], w_ref[...], preferred_element_type=F32)

    @pl.when(j < rot_blocks)
    def _():
        o_ref[...] = _rope(acc, cos_ref, sa_ref, sb_ref).astype(o_ref.dtype)

    @pl.when(j >= rot_blocks)
    def _():
        o_ref[...] = acc.astype(o_ref.dtype)


def _token_tile(n_tokens, want):
    tm = min(want, n_tokens)
    assert n_tokens % tm == 0
    return tm


def _project(x, w, *, rope=None, rot_cols=0, tn=512, out_dtype=BF16):
    t, k = x.shape
    n = w.shape[1]
    tm = _token_tile(t, 1024)
    grid = (t // tm, n // tn)
    x_spec = pl.BlockSpec((tm, k), lambda i, j: (i, 0))
    w_spec = pl.BlockSpec((k, tn), lambda i, j: (0, j))
    o_spec = pl.BlockSpec((tm, tn), lambda i, j: (i, j))
    params = pltpu.CompilerParams(dimension_semantics=("parallel", "arbitrary"), vmem_limit_bytes=VMEM_LIMIT)
    if rope is None:
        return pl.pallas_call(
            _mm_kernel, grid=grid, in_specs=[x_spec, w_spec], out_specs=o_spec,
            out_shape=jax.ShapeDtypeStruct((t, n), out_dtype), compiler_params=params, name="proj",
        )(x, w)
    assert rot_cols % tn == 0
    tab_spec = pl.BlockSpec((tm, LANES), lambda i, j: (i, 0))
    return pl.pallas_call(
        functools.partial(_mm_rope_kernel, rot_blocks=rot_cols // tn), grid=grid,
        in_specs=[x_spec, w_spec, tab_spec, tab_spec, tab_spec], out_specs=o_spec,
        out_shape=jax.ShapeDtypeStruct((t, n), out_dtype), compiler_params=params, name="proj_rope",
    )(x, w, *rope)


def _rope(acc, cos_ref, sa_ref, sb_ref):
    tn = acc.shape[1]
    reps = tn // LANES
    cos = jnp.concatenate([cos_ref[...]] * reps, axis=1)
    sa = jnp.concatenate([sa_ref[...]] * reps, axis=1)
    sb = jnp.concatenate([sb_ref[...]] * reps, axis=1)
    half = ROT_DIM // 2
    return acc * cos + pltpu.roll(acc, tn - half, 1) * sa + pltpu.roll(acc, half, 1) * sb


def _mm_deint_kernel(*refs, dil, rope):
    if rope:
        x_ref, w_ref, cos_ref, sa_ref, sb_ref, o_ref = refs[:6]
    else:
        x_ref, w_ref, o_ref = refs[:3]
    acc = jnp.dot(x_ref[...], w_ref[...], preferred_element_type=F32)
    if rope:
        acc = _rope(acc, cos_ref, sa_ref, sb_ref)
    if dil == 1:
        o_ref[0] = acc.astype(o_ref.dtype)
    else:
        acc_scr = refs[-1]
        rows = acc.shape[0] // dil
        for c in range(acc.shape[1] // LANES):
            sl = slice(c * LANES, (c + 1) * LANES)
            acc_scr[c] = acc[:, sl]
            for r in range(dil):
                o_ref[r, :, sl] = acc_scr[c, pl.ds(r, rows, stride=dil), :].astype(o_ref.dtype)


def _project_deint(x, w, col, dil, rope=None):
    t, k = x.shape
    tm = _token_tile(t, 512)
    wn = D_MODEL
    in_specs = [pl.BlockSpec((tm, k), lambda i: (i, 0)), pl.BlockSpec((k, wn), lambda i: (0, col))]
    args = [x, w]
    if rope is not None:
        tab_blocks = rope[0].shape[0] // tm
        in_specs += [pl.BlockSpec((tm, LANES), lambda i: (i % tab_blocks, 0))] * 3
        args += list(rope)
    return pl.pallas_call(
        functools.partial(_mm_deint_kernel, dil=dil, rope=rope is not None), grid=(t // tm,),
        in_specs=in_specs, out_specs=pl.BlockSpec((dil, tm // dil, wn), lambda i: (0, i, 0)),
        out_shape=jax.ShapeDtypeStruct((dil, t // dil, wn), BF16),
        scratch_shapes=[pltpu.VMEM((wn // LANES, tm, LANES), F32)] if dil > 1 else [],
        compiler_params=pltpu.CompilerParams(dimension_semantics=("parallel",), vmem_limit_bytes=VMEM_LIMIT),
        name=f"proj_deint{dil}",
    )(*args)


GROUPS_PER_SLAB = LANES // SSM_GROUP


def _mm_ssm_kernel(x_ref, w_ref, xg_ref, z_ref, slab_scr, grp_scr, *, chunk):
    tm = x_ref.shape[0]
    n_chunks = tm // chunk
    x = x_ref[...]
    z_ref[...] = jnp.dot(x, w_ref[:, D_MODEL:], preferred_element_type=F32).astype(z_ref.dtype)
    u = jnp.dot(x, w_ref[:, :D_MODEL], preferred_element_type=F32)
    for c in range(D_MODEL // LANES):
        slab_scr[c] = u[:, c * LANES:(c + 1) * LANES]
    for s in range(chunk):
        for c in range(D_MODEL // LANES):
            rows = slab_scr[c, pl.ds(s, n_chunks, stride=chunk), :]
            for gg in range(GROUPS_PER_SLAB):
                grp_scr[c * GROUPS_PER_SLAB + gg, :, s * SSM_GROUP:(s + 1) * SSM_GROUP] = (
                    rows[:, gg * SSM_GROUP:(gg + 1) * SSM_GROUP])
    xg_ref[...] = grp_scr[...].astype(xg_ref.dtype)


def _project_ssm(x, w, chunk):
    t, k = x.shape
    tm = _token_tile(t, 512)
    cw = chunk * SSM_GROUP
    return pl.pallas_call(
        functools.partial(_mm_ssm_kernel, chunk=chunk), grid=(t // tm,),
        in_specs=[pl.BlockSpec((tm, k), lambda i: (i, 0)), pl.BlockSpec(w.shape, lambda i: (0, 0))],
        out_specs=[pl.BlockSpec((N_SSM_GROUPS, tm // chunk, cw), lambda i: (0, i, 0)),
                   pl.BlockSpec((tm, D_MODEL), lambda i: (i, 0))],
        out_shape=[jax.ShapeDtypeStruct((N_SSM_GROUPS, t // chunk, cw), BF16),
                   jax.ShapeDtypeStruct((t, D_MODEL), BF16)],
        scratch_shapes=[pltpu.VMEM((D_MODEL // LANES, tm, LANES), F32),
                        pltpu.VMEM((N_SSM_GROUPS, tm // chunk, cw), F32)],
        compiler_params=pltpu.CompilerParams(dimension_semantics=("parallel",), vmem_limit_bytes=VMEM_LIMIT),
        name=f"proj_ssm{chunk}",
    )(x, w)


def _rope_tables(pos, scale):
    half = ROT_DIM // 2
    inv_freq = ROPE_THETA ** (-jnp.arange(0, ROT_DIM, 2, dtype=F32) / ROT_DIM)
    ang = pos[:, None] * inv_freq[None, :]
    cos, sin = jnp.cos(ang), jnp.sin(ang)
    t = pos.shape[0]
    ones = jnp.ones((t, HEAD_DIM - ROT_DIM), F32)
    zeros = jnp.zeros((t, HEAD_DIM - ROT_DIM), F32)
    zhalf = jnp.zeros((t, half), F32)
    cos64 = jnp.concatenate([cos, cos, ones], axis=1)
    sa64 = jnp.concatenate([-sin, zhalf, zeros], axis=1)
    sb64 = jnp.concatenate([zhalf, sin, zeros], axis=1)
    return tuple(scale * jnp.concatenate([a, a], axis=1) for a in (cos64, sa64, sb64))


def _ssm_operators(a_re, a_im, log_dt, b_re, b_im, c_re, c_im, d_skip):
    g, p, m, c = N_SSM_GROUPS, SSM_STATE, SSM_GROUP, SSM_CHUNK
    hi = lax.Precision.HIGHEST
    a = lax.complex(a_re.astype(F32), a_im.astype(F32))
    dta = jnp.exp(log_dt.astype(F32))[:, None] * a
    a_bar = jnp.exp(dta)
    b_bar = ((a_bar - 1.0) / a)[..., None] * lax.complex(b_re.astype(F32), b_im.astype(F32))
    cc = lax.complex(c_re.astype(F32), c_im.astype(F32))
    pw = jnp.exp(jnp.arange(c + 1, dtype=F32)[:, None, None] * dta[None])

    cp = cc[None] * pw[:c, :, None, :]
    kern = (jnp.einsum("tgmp,gpn->tgmn", cp.real, b_bar.real, precision=hi)
            - jnp.einsum("tgmp,gpn->tgmn", cp.imag, b_bar.imag, precision=hi))
    s_idx = jnp.arange(c)[:, None]
    t_idx = jnp.arange(c)[None, :]
    lag = t_idx - s_idx
    blocks = kern[jnp.clip(lag, 0, c - 1)]
    blocks = jnp.where((lag >= 0)[:, :, None, None, None], blocks, 0.0)
    t_op = blocks.transpose(2, 0, 4, 1, 3).reshape(g, c * m, c * m)
    d_rep = jnp.tile(d_skip.astype(F32).reshape(g, m), (1, c))
    t_op = t_op + jnp.eye(c * m, dtype=F32)[None] * d_rep[:, None, :]

    bin_c = pw[c - 1 - jnp.arange(c)][:, :, :, None] * b_bar[None]
    bin_c = bin_c.transpose(1, 0, 3, 2).reshape(g, c * m, p)
    b_in = jnp.concatenate([bin_c.real, bin_c.imag], axis=-1)

    w = cc[None] * pw[1:, :, None, :]
    w = w.transpose(1, 3, 0, 2).reshape(g, p, c * m)
    c_out = jnp.concatenate([w.real, -w.imag], axis=1)

    steps = jnp.array([16, 32, 64, 128, 8, 0, 0, 0] + [16 * r for r in range(SUBLANES)], F32)
    pows = jnp.exp(steps[None, :, None] * dta[:, None, :])
    ca = jnp.concatenate([pows.real, pows.real], axis=-1)
    cb = jnp.concatenate([-pows.imag, pows.imag], axis=-1)
    return t_op.astype(BF16), b_in.astype(BF16), c_out.astype(BF16), ca, cb


def _swap_halves(x):
    return pltpu.roll(x, SSM_STATE, x.ndim - 1)


def _cmul(x, xs, ca, cb):
    return x * ca + xs * cb


def _ssm_kernel(x_ref, xs_ref, h0_ref, t_ref, bin_ref, cout_ref, ca_ref, cb_ref,
                y_ref, hl_ref, ys_ref, hs_ref, l_scr, ls_scr, e_scr, *, n_seq, rows_per_seq):
    rows = n_seq * rows_per_seq
    n_blocks = rows_per_seq // SUBLANES
    ca = ca_ref[0]
    cb = cb_ref[0]
    x = x_ref[0]
    v = jnp.dot(x, bin_ref[0], preferred_element_type=F32)
    rmod = lax.broadcasted_iota(jnp.int32, (rows, LANES), 0) & (SUBLANES - 1)

    for k, dist in enumerate((1, 2, 4)):
        sh = jnp.where(rmod >= dist, pltpu.roll(v, dist, 0), 0.0)
        v = v + _cmul(sh, _swap_halves(sh), ca[k:k + 1], cb[k:k + 1])
    l_scr[...] = v
    ls_scr[...] = _swap_halves(v)

    ca128 = ca[3:4]
    cb128 = cb[3:4]
    cb128s = _swap_halves(cb128)

    def body(b, carry):
        e, es = carry
        lasts, lasts_s = [], []
        for s in range(n_seq):
            base = s * rows_per_seq + b * SUBLANES
            e_scr[pl.ds(base, SUBLANES), :] = jnp.broadcast_to(e[s:s + 1], (SUBLANES, LANES))
            lasts.append(l_scr[pl.ds(base + SUBLANES - 1, 1), :])
            lasts_s.append(ls_scr[pl.ds(base + SUBLANES - 1, 1), :])
        e_new = jnp.concatenate(lasts, axis=0) + e * ca128 + es * cb128
        es_new = jnp.concatenate(lasts_s, axis=0) + es * ca128 + e * cb128s
        return e_new, es_new

    zero = jnp.zeros((n_seq, LANES), F32)
    e_fin, _ = lax.fori_loop(0, n_blocks, body, (zero, zero))
    hl_ref[0] = e_fin

    lshift = jnp.where(rmod >= 1, pltpu.roll(v, 1, 0), 0.0)
    ebc = e_scr[...]
    pwa = ca[SUBLANES:2 * SUBLANES]
    pwb = cb[SUBLANES:2 * SUBLANES]
    shape3 = (rows // SUBLANES, SUBLANES, LANES)
    h = (lshift.reshape(shape3) + ebc.reshape(shape3) * pwa[None]
         + _swap_halves(ebc).reshape(shape3) * pwb[None]).reshape(rows, LANES)
    y = jnp.dot(x, t_ref[0], preferred_element_type=F32)
    y = y + jnp.dot(h.astype(BF16), cout_ref[0], preferred_element_type=F32)
    y_ref[0] = y.astype(y_ref.dtype)

    half = SSM_CHUNK * SSM_GROUP // 2
    xs = xs_ref[0]
    h0 = h0_ref[0]
    ys = jnp.dot(xs, t_ref[0, :half, :half], preferred_element_type=F32)
    ys = ys + jnp.dot(h0.astype(BF16), cout_ref[0, :, :half], preferred_element_type=F32)
    ys_ref[0] = ys.astype(ys_ref.dtype)
    hs = _cmul(h0, _swap_halves(h0), ca[4:5], cb[4:5])
    hs_ref[0] = hs + jnp.dot(xs, bin_ref[0, half:, :], preferred_element_type=F32)


def _ssm(xg, xsg, h0g, ops, *, n_seq):
    t_op, b_in, c_out, ca, cb = ops
    g, rows, cw = xg.shape
    n_s = xsg.shape[1]
    hw = xsg.shape[2]

    def spec(shape):
        return pl.BlockSpec((1,) + shape, lambda i: (i, 0, 0))

    return pl.pallas_call(
        functools.partial(_ssm_kernel, n_seq=n_seq, rows_per_seq=rows // n_seq),
        grid=(g,),
        in_specs=[spec((rows, cw)), spec((n_s, hw)), spec((n_s, LANES)), spec((cw, cw)), spec((cw, LANES)),
                  spec((LANES, cw)), spec((2 * SUBLANES, LANES)), spec((2 * SUBLANES, LANES))],
        out_specs=[spec((rows, cw)), spec((n_seq, LANES)), spec((n_s, hw)), spec((n_s, LANES))],
        out_shape=[jax.ShapeDtypeStruct((g, rows, cw), BF16), jax.ShapeDtypeStruct((g, n_seq, LANES), F32),
                   jax.ShapeDtypeStruct((g, n_s, hw), BF16), jax.ShapeDtypeStruct((g, n_s, LANES), F32)],
        scratch_shapes=[pltpu.VMEM((rows, LANES), F32)] * 3,
        compiler_params=pltpu.CompilerParams(dimension_semantics=("parallel",), vmem_limit_bytes=VMEM_LIMIT),
        name="s5_scan",
    )(xg, xsg, h0g, t_op, b_in, c_out, ca, cb)


def _post_layer(x, sub, p, g_ref, b_ref, wpg_ref, wpe_ref, alpha):
    r = alpha * x + sub
    mu = jnp.mean(r, axis=-1, keepdims=True)
    cen = r - mu
    var = jnp.mean(cen * cen, axis=-1, keepdims=True)
    h = cen * lax.rsqrt(var + LN_EPS) * g_ref[...] + b_ref[...]
    gate = _sigmoid(jnp.dot(h.astype(BF16), wpg_ref[...], preferred_element_type=F32))
    ple = jnp.dot(p.astype(BF16), wpe_ref[...], preferred_element_type=F32)
    return h + gate * ple


def _glu_post_kernel(yg_ref, z_ref, x_ref, p_ref, g_ref, b_ref, wglu_ref, wout_ref, wpg_ref, wpe_ref,
                     xo_ref, xb_ref, grp_scr, stage_scr, slab_scr, *, alpha, chunk):
    n_chunks = yg_ref.shape[1]
    grp_scr[...] = yg_ref[...].astype(F32)
    for t in range(chunk):
        for c in range(D_MODEL // LANES):
            for gg in range(GROUPS_PER_SLAB):
                stage_scr[:, gg * SSM_GROUP:(gg + 1) * SSM_GROUP] = (
                    grp_scr[c * GROUPS_PER_SLAB + gg, :, t * SSM_GROUP:(t + 1) * SSM_GROUP])
            slab_scr[c, pl.ds(t, n_chunks, stride=chunk), :] = stage_scr[...]
    y = jnp.concatenate([slab_scr[c] for c in range(D_MODEL // LANES)], axis=1)
    g = _gelu_tanh(y)
    t = jnp.dot(g.astype(BF16), wglu_ref[...], preferred_element_type=F32)
    z = z_ref[...].astype(F32)
    glu = g * _sigmoid(t) * (z * _sigmoid(z))
    sub = jnp.dot(glu.astype(BF16), wout_ref[...], preferred_element_type=F32)
    out = _post_layer(x_ref[...], sub, p_ref[...], g_ref, b_ref, wpg_ref, wpe_ref, alpha)
    xo_ref[...] = out
    xb_ref[...] = out.astype(BF16)


def _interleave_rows(ref, scr, dil):
    if dil == 1:
        return ref[0].astype(F32)
    rows = ref.shape[1]
    slabs = []
    for c in range(ref.shape[2] // LANES):
        for r in range(dil):
            scr[c, pl.ds(r, rows, stride=dil), :] = ref[r, :, c * LANES:(c + 1) * LANES].astype(F32)
        slabs.append(scr[c])
    return slabs[0] if len(slabs) == 1 else jnp.concatenate(slabs, axis=1)


def _comb_post_kernel(*refs, dils, alpha):
    n = len(dils)
    if n == 1:
        o_ref = refs[0]
        z_ref, x_ref, p_ref, g_ref, b_ref, wout_ref, wpg_ref, wpe_ref, xo_ref, xb_ref = refs[1:]
        o = o_ref[...].astype(F32)
    else:
        o_refs, lse_refs = refs[:n], refs[n:2 * n]
        z_ref, x_ref, p_ref, g_ref, b_ref, wout_ref, wpg_ref, wpe_ref, xo_ref, xb_ref = refs[2 * n:2 * n + 10]
        scrs = list(refs[2 * n + 10:])
        lses = [_interleave_rows(r, scrs.pop(0) if d > 1 else None, d) for r, d in zip(lse_refs, dils)]
        top = functools.reduce(jnp.maximum, lses)
        ws = [jnp.exp(l - top) for l in lses]
        tot = functools.reduce(lambda a, c: a + c, ws)
        src = lax.broadcasted_iota(jnp.int32, (LANES, ATTN_WIDTH), 0)
        dst = lax.broadcasted_iota(jnp.int32, (LANES, ATTN_WIDTH), 1)
        lanes_per_head = LANES // N_HEADS
        expand = jnp.where(src == (dst >> (HEAD_DIM.bit_length() - 1)) * lanes_per_head, 1.0, 0.0).astype(BF16)
        o = None
        for w, r, d in zip(ws, o_refs, dils):
            a = w / tot
            hi = a.astype(BF16)
            lo = (a - hi.astype(F32)).astype(BF16)
            wide = (jnp.dot(hi, expand, preferred_element_type=F32) + jnp.dot(lo, expand, preferred_element_type=F32))
            term = wide * _interleave_rows(r, scrs.pop(0) if d > 1 else None, d)
            o = term if o is None else o + term
    z = z_ref[...].astype(F32)
    gated = o * (z * _sigmoid(z))
    sub = jnp.dot(gated.astype(BF16), wout_ref[...], preferred_element_type=F32)
    out = _post_layer(x_ref[...], sub, p_ref[...], g_ref, b_ref, wpg_ref, wpe_ref, alpha)
    xo_ref[...] = out
    xb_ref[...] = out.astype(BF16)


def _tok_spec(tm, width, col=0):
    return pl.BlockSpec((tm, width), lambda i: (i, col))


def _full_spec(shape):
    return pl.BlockSpec(shape, lambda i: (0,) * len(shape))


def _post_outputs(t, tm):
    return dict(
        out_specs=[_tok_spec(tm, D_MODEL), _tok_spec(tm, D_MODEL)],
        out_shape=[jax.ShapeDtypeStruct((t, D_MODEL), F32), jax.ShapeDtypeStruct((t, D_MODEL), BF16)],
        compiler_params=pltpu.CompilerParams(dimension_semantics=("parallel",), vmem_limit_bytes=VMEM_LIMIT),
    )


def _glu_post(yg, z, x, p, layer, ln_g, ln_b, w_glu, w_out, w_pg, w_pe, alpha):
    t = x.shape[0]
    tm = _token_tile(t, 512)
    ple = p.shape[2]
    n_grp, _, cw = yg.shape
    chunk = cw // SSM_GROUP
    return pl.pallas_call(
        functools.partial(_glu_post_kernel, alpha=alpha, chunk=chunk), grid=(t // tm,),
        in_specs=[pl.BlockSpec((n_grp, tm // chunk, cw), lambda i: (0, i, 0)), _tok_spec(tm, D_MODEL),
                  _tok_spec(tm, D_MODEL), pl.BlockSpec((None, tm, ple), lambda i: (layer, i, 0)),
                  _full_spec((1, D_MODEL)), _full_spec((1, D_MODEL)), _full_spec((D_MODEL, D_MODEL)),
                  _full_spec((D_MODEL, D_MODEL)), _full_spec((D_MODEL, D_MODEL)), _full_spec((ple, D_MODEL))],
        scratch_shapes=[pltpu.VMEM((n_grp, tm // chunk, cw), F32), pltpu.VMEM((tm // chunk, LANES), F32),
                        pltpu.VMEM((D_MODEL // LANES, tm, LANES), F32)],
        name="glu_post", **_post_outputs(t, tm),
    )(yg, z, x, p, ln_g, ln_b, w_glu, w_out, w_pg, w_pe)


def _comb_post(os_, lses, dils, z, z_col, x, p, layer, ln_g, ln_b, w_out, w_pg, w_pe, alpha):
    t = x.shape[0]
    tm = _token_tile(t, 512)
    ple = p.shape[2]
    if len(dils) == 1:
        group_specs = [_tok_spec(tm, D_MODEL)]
        scratch = []
    else:
        group_specs = [pl.BlockSpec((d, tm // d, D_MODEL), lambda i: (0, i, 0)) for d in dils]
        group_specs += [pl.BlockSpec((d, tm // d, LANES), lambda i: (0, i, 0)) for d in dils]
        scratch = [pltpu.VMEM((1, tm, LANES), F32) for d in dils if d > 1]
        scratch += [pltpu.VMEM((D_MODEL // LANES, tm, LANES), F32) for d in dils if d > 1]
    return pl.pallas_call(
        functools.partial(_comb_post_kernel, dils=tuple(dils), alpha=alpha), grid=(t // tm,),
        in_specs=group_specs
        + [_tok_spec(tm, D_MODEL, z_col), _tok_spec(tm, D_MODEL), pl.BlockSpec((None, tm, ple), lambda i: (layer, i, 0)),
           _full_spec((1, D_MODEL)), _full_spec((1, D_MODEL)), _full_spec((D_MODEL, D_MODEL)),
           _full_spec((D_MODEL, D_MODEL)), _full_spec((ple, D_MODEL))],
        scratch_shapes=scratch, name="attn_out_post", **_post_outputs(t, tm),
    )(*os_, *lses, z, x, p, ln_g, ln_b, w_out, w_pg, w_pe)


def _attn_prompt_kernel(q_ref, kp_ref, kc_ref, vp_ref, vc_ref, o_ref, lse_ref):
    ib = pl.program_id(2)
    tq = q_ref.shape[0]
    row = lax.broadcasted_iota(jnp.int32, (tq, 2 * tq), 0)
    col = lax.broadcasted_iota(jnp.int32, (tq, 2 * tq), 1)
    back = row + tq - col
    valid = (back >= 0) & (back <= WINDOW_KEYS) & ((col >= tq) | (ib > 0))
    lane = lax.broadcasted_iota(jnp.int32, (tq, LANES), 1)
    low = lane < HEAD_DIM
    head_of_lane = lane >> ((LANES // N_HEADS).bit_length() - 1)
    lse_tile = jnp.zeros((tq, LANES), F32)
    nt = (((1,), (1,)), ((), ()))
    for j in range(ATTN_WIDTH // LANES):
        sl = slice(j * LANES, (j + 1) * LANES)
        q2 = q_ref[:, sl]
        k2 = jnp.concatenate([kp_ref[:, sl], kc_ref[:, sl]], axis=0)
        v2 = jnp.concatenate([vp_ref[:, sl], vc_ref[:, sl]], axis=0)
        outs = []
        for sel in (low, jnp.logical_not(low)):
            qm = jnp.where(sel, q2, jnp.zeros_like(q2))
            s = lax.dot_general(qm, k2, nt, preferred_element_type=F32)
            s = jnp.where(valid, s, NEG_INF)
            m = jnp.max(s, axis=1, keepdims=True)
            pr = jnp.exp(s - m)
            l = jnp.sum(pr, axis=1, keepdims=True)
            o = jnp.dot(pr.astype(BF16), v2, preferred_element_type=F32)
            outs.append((o / l, m + jnp.log(l)))
        o_ref[:, sl] = jnp.where(low, outs[0][0], outs[1][0]).astype(o_ref.dtype)
        for hh in range(2):
            lse_tile = jnp.where(head_of_lane == 2 * j + hh, outs[hh][1], lse_tile)
    lse_ref[...] = lse_tile


def _attn_prompt_group(q, k, v, n_seq, seq_len):
    dil, rows, _ = q.shape
    tq = WINDOW_KEYS
    nb = seq_len // dil // tq
    assert rows == n_seq * nb * tq

    def cur(b, r, ib):
        return (r, b * nb + ib, 0)

    def prev(b, r, ib):
        return (r, b * nb + jnp.maximum(ib - 1, 0), 0)

    blk = (None, tq, D_MODEL)
    return pl.pallas_call(
        _attn_prompt_kernel, grid=(n_seq, dil, nb),
        in_specs=[pl.BlockSpec(blk, cur), pl.BlockSpec(blk, prev), pl.BlockSpec(blk, cur),
                  pl.BlockSpec(blk, prev), pl.BlockSpec(blk, cur)],
        out_specs=[pl.BlockSpec(blk, cur), pl.BlockSpec((None, tq, LANES), cur)],
        out_shape=[jax.ShapeDtypeStruct((dil, rows, D_MODEL), BF16), jax.ShapeDtypeStruct((dil, rows, LANES), F32)],
        compiler_params=pltpu.CompilerParams(dimension_semantics=("parallel", "parallel", "arbitrary"),
                                             vmem_limit_bytes=VMEM_LIMIT),
        name=f"attn_prompt_d{dil}",
    )(q, k, k, v, v)


SAMPLE_KEY_CHUNK = 512


def _attn_sample_kernel(q_ref, kvn_ref, c0_ref, c1_ref, c2_ref, o_ref, s_scr, m_scr):
    step = pl.program_id(1)
    ds = q_ref.shape[1]
    n_rows = N_HEADS * ds
    assert ds & (ds - 1) == 0 and n_rows == LANES
    rho = lax.broadcasted_iota(jnp.int32, (n_rows, ATTN_WIDTH), 0)
    lane = lax.broadcasted_iota(jnp.int32, (n_rows, ATTN_WIDTH), 1)
    head_mask = (rho >> (ds.bit_length() - 1)) == (lane >> (HEAD_DIM.bit_length() - 1))
    cache_refs = (c0_ref, c1_ref, c2_ref)
    nt = (((1,), (1,)), ((), ()))
    kvn = kvn_ref[0].astype(F32)

    def new_block(g, off):
        piece = kvn[:, off + g * ATTN_WIDTH: off + (g + 1) * ATTN_WIDTH]
        pad = jnp.zeros((LANES - ds, ATTN_WIDTH), F32)
        return jnp.concatenate([piece, pad], axis=0).astype(BF16)

    chunks = []
    col = 0
    for g, c_ref in enumerate(cache_refs):
        width = c_ref.shape[3]
        for c0 in range(0, width, SAMPLE_KEY_CHUNK):
            cw = min(SAMPLE_KEY_CHUNK, width - c0)
            chunks.append((g, c0, cw, col))
            col += cw
    new_cols = [col + g * LANES for g in range(N_ATT_GROUPS)]

    @pl.when(step == 0)
    def _():
        q_all = q_ref[0].astype(F32)
        q_rows = []
        for g in range(N_ATT_GROUPS):
            qg = q_all[:, g * ATTN_WIDTH:(g + 1) * ATTN_WIDTH]
            q_rows.append(jnp.where(head_mask, jnp.concatenate([qg] * N_HEADS, axis=0), 0.0).astype(BF16))
        m = jnp.full((n_rows, 1), NEG_INF, F32)
        for g, c0, cw, col0 in chunks:
            kt = cache_refs[g][0, 0, :, c0:c0 + cw].astype(BF16)
            s = jnp.dot(q_rows[g], kt, preferred_element_type=F32)
            qi = lax.broadcasted_iota(jnp.int32, (n_rows, cw), 0) & (ds - 1)
            back = lax.broadcasted_iota(jnp.int32, (n_rows, cw), 1) + c0 - qi
            s = jnp.where((back >= 0) & ((back & (DILATIONS[g] - 1)) == 0), s, NEG_INF)
            s_scr[:, col0:col0 + cw] = s
            m = jnp.maximum(m, jnp.max(s, axis=1, keepdims=True))
        qi = lax.broadcasted_iota(jnp.int32, (n_rows, LANES), 0) & (ds - 1)
        kappa = lax.broadcasted_iota(jnp.int32, (n_rows, LANES), 1)
        for g in range(N_ATT_GROUPS):
            s = lax.dot_general(q_rows[g], new_block(g, 0), nt, preferred_element_type=F32)
            ahead = qi - kappa
            s = jnp.where((ahead >= 0) & ((ahead & (DILATIONS[g] - 1)) == 0) & (kappa < ds), s, NEG_INF)
            s_scr[:, new_cols[g]:new_cols[g] + LANES] = s
            m = jnp.maximum(m, jnp.max(s, axis=1, keepdims=True))
        m_scr[...] = jnp.broadcast_to(m, m_scr.shape)

    @pl.when(step == 1)
    def _():
        m = m_scr[:, 0:1]
        l = jnp.zeros((n_rows, 1), F32)
        acc = jnp.zeros((n_rows, ATTN_WIDTH), F32)
        for g, c0, cw, col0 in chunks:
            pr = jnp.exp(s_scr[:, col0:col0 + cw] - m)
            l = l + jnp.sum(pr, axis=1, keepdims=True)
            vt = cache_refs[g][0, 0, :, c0:c0 + cw].astype(BF16)
            acc = acc + lax.dot_general(pr.astype(BF16), vt, nt, preferred_element_type=F32)
        for g in range(N_ATT_GROUPS):
            pr = jnp.exp(s_scr[:, new_cols[g]:new_cols[g] + LANES] - m)
            l = l + jnp.sum(pr, axis=1, keepdims=True)
            acc = acc + jnp.dot(pr.astype(BF16), new_block(g, N_ATT_GROUPS * ATTN_WIDTH),
                                preferred_element_type=F32)
        acc = acc / l
        acc3 = acc.reshape(N_HEADS, ds, ATTN_WIDTH)
        low = lax.broadcasted_iota(jnp.int32, (ds, LANES), 1) < HEAD_DIM
        for c in range(ATTN_WIDTH // LANES):
            sl = slice(c * LANES, (c + 1) * LANES)
            o_ref[0, :, sl] = jnp.where(low, acc3[2 * c][:, sl], acc3[2 * c + 1][:, sl])


def _cache_views(caches):
    views = []
    for cache, dil in zip(caches, DILATIONS):
        n, width = cache.shape[:2]
        assert width == WINDOW_KEYS * dil and cache.shape[2:] == (2, N_HEADS, HEAD_DIM)
        views.append(cache.transpose(0, 2, 3, 4, 1).reshape(n, 2, ATTN_WIDTH, width))
    return views


def _attn_sample(qz_s, kv_s, views, n_seq, ds):
    q3 = qz_s.reshape(n_seq, ds, qz_s.shape[1])
    kv3 = kv_s.reshape(n_seq, ds, kv_s.shape[1])
    n_cols = sum(v.shape[3] for v in views) + N_ATT_GROUPS * LANES

    def tok_spec(width):
        return pl.BlockSpec((1, ds, width), lambda n, s: (n, 0, 0))

    o = pl.pallas_call(
        _attn_sample_kernel, grid=(n_seq, 2),
        in_specs=[tok_spec(q3.shape[2]), tok_spec(kv3.shape[2])]
        + [pl.BlockSpec((1, 1, ATTN_WIDTH, v.shape[3]), lambda n, s: (n, s, 0, 0)) for v in views],
        out_specs=tok_spec(ATTN_WIDTH),
        out_shape=jax.ShapeDtypeStruct((n_seq, ds, ATTN_WIDTH), F32),
        scratch_shapes=[pltpu.VMEM((N_HEADS * ds, n_cols), F32), pltpu.VMEM((N_HEADS * ds, LANES), F32)],
        compiler_params=pltpu.CompilerParams(dimension_semantics=("parallel", "arbitrary"),
                                             vmem_limit_bytes=VMEM_LIMIT),
        name="attn_sample",
    )(q3, kv3, *views)
    return o.reshape(n_seq * ds, ATTN_WIDTH)


def kernel(x_prompt, x_sample, state_ssm_re, state_ssm_im, cache_kv_w128, cache_kv_w512, cache_kv_w2048, p_prompt, p_sample, ln_g, ln_b, w_pe, w_pg, w_in_a, a_re, a_im, log_dt, b_re, b_im, c_re, c_im, d_skip, w_glu, w_out_a, w_kv, w_in_b, w_out_b):
    depth = ln_g.shape[0]
    n_a = w_in_a.shape[0]
    batch, seq, _ = x_prompt.shape
    n_s, ds, _ = x_sample.shape
    past = cache_kv_w2048.shape[1]
    assert ds * 2 == SSM_CHUNK and ds <= DILATIONS[2] and past == WINDOWS[2] and seq % (DILATIONS[2] * WINDOW_KEYS) == 0
    alpha = (2 * depth) ** 0.25
    t_p, t_s = batch * seq, n_s * ds
    caches =(cache_kv_w128, cache_kv_w512, cache_kv_w2048)

    xs = [x_prompt.reshape(t_p, D_MODEL), x_sample.reshape(t_s, D_MODEL)]
    xbs = [v.astype(BF16) for v in xs]
    ps = [p_prompt.reshape(depth, t_p, -1), p_sample.reshape(depth, t_s, -1)]
    wb = lambda w: w.astype(BF16)
    h_prompt, h_sample = [], []

    for i in range(n_a):
        w_in = wb(w_in_a[i])
        (xg, z_p), (xsg, z_s) = _project_ssm(xbs[0], w_in, SSM_CHUNK), _project_ssm(xbs[1], w_in, ds)
        ops = _ssm_operators(a_re[i], a_im[i], log_dt[i], b_re[i], b_im[i], c_re[i], c_im[i], d_skip[i])
        h0g = jnp.concatenate([state_ssm_re[i], state_ssm_im[i]], axis=-1).astype(F32).transpose(1, 0, 2)
        yg, hl, ysg, hs = _ssm(xg, xsg, h0g, ops, n_seq=batch)
        h_prompt.append(hl.transpose(1, 0, 2))
        h_sample.append(hs.transpose(1, 0, 2))
        for k, (y, z) in enumerate(((yg, z_p), (ysg, z_s))):
            xs[k], xbs[k] = _glu_post(y, z, xs[k], ps[k], i, ln_g[i:i + 1], ln_b[i:i + 1], wb(w_glu[i]),
                                      wb(w_out_a[i]), wb(w_pg[i]), wb(w_pe[i]), alpha)

    pos = [jnp.arange(seq, dtype=F32), jnp.tile(past + jnp.arange(ds, dtype=F32), n_s)]
    rope_k = [_rope_tables(p_, 1.0) for p_ in pos]
    rope_q = [tuple(HEAD_DIM ** -0.5 * tab for tab in tabs) for tabs in rope_k]
    q_width = N_ATT_GROUPS * ATTN_WIDTH
    n_grp = N_ATT_GROUPS
    w_kv_b = wb(w_kv)
    k_p = [_project_deint(xbs[0], w_kv_b, grp, DILATIONS[grp], rope_k[0]) for grp in range(n_grp)]
    v_p = [_project_deint(xbs[0], w_kv_b, n_grp + grp, DILATIONS[grp]) for grp in range(n_grp)]
    kv_s = _project(xbs[1], w_kv_b, rope=rope_k[1], rot_cols=q_width, tn=512)
    views = _cache_views(caches)

    for i in range(n_a, depth):
        j = i - n_a
        w_in = wb(w_in_b[j])
        post_w = (ln_g[i:i + 1], ln_b[i:i + 1], wb(w_out_b[j]), wb(w_pg[i]), wb(w_pe[i]), alpha)
        q_p = [_project_deint(xbs[0], w_in, grp, DILATIONS[grp], rope_q[0]) for grp in range(n_grp)]
        z_p = _project_deint(xbs[0], w_in, n_grp, 1).reshape(t_p, D_MODEL)
        parts = [_attn_prompt_group(q_p[grp], k_p[grp], v_p[grp], batch, seq) for grp in range(n_grp)]
        qz_s = _project(xbs[1], w_in, rope=rope_q[1], rot_cols=q_width, tn=512)
        o_s = _attn_sample(qz_s, kv_s, views, n_s, ds)
        xs[0], xbs[0] = _comb_post([pt[0] for pt in parts], [pt[1] for pt in parts], DILATIONS, z_p, 0,
                                   xs[0], ps[0], i, *post_w)
        xs[1], xbs[1] = _comb_post([o_s], [], (1,), qz_s, n_grp, xs[1], ps[1], i, *post_w)

    y_prompt = xs[0].reshape(batch, seq, D_MODEL)
    y_sample = xs[1].reshape(n_s, ds, D_MODEL)
    hp = jnp.stack(h_prompt)
    hsm = jnp.stack(h_sample)

    def prompt_window(arr, dil, width):
        per = arr.reshape(dil, batch, seq // dil, ATTN_WIDTH)[:, :, (seq - width) // dil:]
        return per.transpose(1, 2, 0, 3).reshape(batch, width, N_HEADS, HEAD_DIM).astype(F32)

    kv_prompt = [jnp.stack([prompt_window(k_p[grp], DILATIONS[grp], min(w, seq)),
                            prompt_window(v_p[grp], DILATIONS[grp], min(w, seq))], axis=2)
                 for grp, w in enumerate(WINDOWS)]
    kv_s6 = kv_s.astype(F32).reshape(n_s, ds, 2, N_ATT_GROUPS, N_HEADS, HEAD_DIM)
    kv_sample = [kv_s6[:, :, :, grp] for grp in range(N_ATT_GROUPS)]
    return (y_prompt, y_sample, hp[..., :SSM_STATE], hp[..., SSM_STATE:], hsm[..., :SSM_STATE], hsm[..., SSM_STATE:],
            *kv_prompt, *kv_sample)
```

```python
import functools
import math

import jax
import jax.numpy as jnp
from jax import lax
from jax.experimental import pallas as pl
from jax.experimental.pallas import tpu as pltpu

F32 = jnp.float32
BF16 = jnp.bfloat16

D_MODEL = 1024
N_SSM_GROUPS = 64
SSM_GROUP = 16
SSM_STATE = 64
SSM_CHUNK = 16
N_ATT_GROUPS = 3
WINDOWS = (128, 512, 2048)
DILATIONS = (1, 4, 16)
HEAD_DIM = 64
N_HEADS = 16
ATTN_WIDTH = N_HEADS * HEAD_DIM
ROT_DIM = HEAD_DIM // 4
ROPE_THETA = 500000.0
WINDOW_KEYS = 128
LN_EPS = 1e-5
NEG_INF = -1e30
LANES = 128
SUBLANES = 8
VMEM_LIMIT = 52 * 1024 * 1024


def _sigmoid(x):
    return 1.0 / (1.0 + jnp.exp(-x))


def _gelu_tanh(x):
    return 0.5 * x * (1.0 + jnp.tanh(math.sqrt(2.0 / math.pi) * (x + 0.044715 * (x * x * x))))


def _mm_kernel(x_ref, w_ref, o_ref):
    o_ref[...] = jnp.dot(x_ref[...], w_ref[...], preferred_element_type=F32).astype(o_ref.dtype)


def _mm_rope_kernel(x_ref, w_ref, cos_ref, sa_ref, sb_ref, o_ref, *, rot_blocks):
    j = pl.program_id(1)
    acc = jnp.dot(x_ref[...], w_ref[...], preferred_element_type=F32)

    @pl.when(j < rot_blocks)
    def _():
        o_ref[...] = _rope(acc, cos_ref, sa_ref, sb_ref).astype(o_ref.dtype)

    @pl.when(j >= rot_blocks)
    def _():
        o_ref[...] = acc.astype(o_ref.dtype)


def _token_tile(n_tokens, want):
    tm = min(want, n_tokens)
    assert n_tokens % tm == 0
    return tm


def _project(x, w, *, rope=None, rot_cols=0, tn=512, out_dtype=BF16):
    t, k = x.shape
    n = w.shape[1]
    tm = _token_tile(t, 1024)
    grid = (t // tm, n // tn)
    x_spec = pl.BlockSpec((tm, k), lambda i, j: (i, 0))
    w_spec = pl.BlockSpec((k, tn), lambda i, j: (0, j))
    o_spec = pl.BlockSpec((tm, tn), lambda i, j: (i, j))
    params = pltpu.CompilerParams(dimension_semantics=("parallel", "arbitrary"), vmem_limit_bytes=VMEM_LIMIT)
    if rope is None:
        return pl.pallas_call(
            _mm_kernel, grid=grid, in_specs=[x_spec, w_spec], out_specs=o_spec,
            out_shape=jax.ShapeDtypeStruct((t, n), out_dtype), compiler_params=params, name="proj",
        )(x, w)
    assert rot_cols % tn == 0
    tab_spec = pl.BlockSpec((tm, LANES), lambda i, j: (i, 0))
    return pl.pallas_call(
        functools.partial(_mm_rope_kernel, rot_blocks=rot_cols // tn), grid=grid,
        in_specs=[x_spec, w_spec, tab_spec, tab_spec, tab_spec], out_specs=o_spec,
        out_shape=jax.ShapeDtypeStruct((t, n), out_dtype), compiler_params=params, name="proj_rope",
    )(x, w, *rope)


def _rope(acc, cos_ref, sa_ref, sb_ref):
    cos, sa, sb = cos_ref[...], sa_ref[...], sb_ref[...]
    half = ROT_DIM // 2
    outs = []
    for c in range(acc.shape[1] // LANES):
        a = acc[:, c * LANES:(c + 1) * LANES]
        outs.append(a * cos + pltpu.roll(a, LANES - half, 1) * sa + pltpu.roll(a, half, 1) * sb)
    return jnp.concatenate(outs, axis=1)


def _mm_deint_kernel(*refs, dil, rope):
    if rope:
        x_ref, w_ref, cos_ref, sa_ref, sb_ref, o_ref = refs[:6]
    else:
        x_ref, w_ref, o_ref = refs[:3]
    acc = jnp.dot(x_ref[...], w_ref[...], preferred_element_type=F32)
    if rope:
        acc = _rope(acc, cos_ref, sa_ref, sb_ref)
    if dil == 1:
        o_ref[0] = acc.astype(o_ref.dtype)
    else:
        acc_scr = refs[-1]
        rows = acc.shape[0] // dil
        for c in range(acc.shape[1] // LANES):
            sl = slice(c * LANES, (c + 1) * LANES)
            acc_scr[c] = acc[:, sl]
            for r in range(dil):
                o_ref[r, :, sl] = acc_scr[c, pl.ds(r, rows, stride=dil), :].astype(o_ref.dtype)


def _project_deint(x, w, col, dil, rope=None):
    t, k = x.shape
    tm = _token_tile(t, 1024)
    wn = D_MODEL
    in_specs = [pl.BlockSpec((tm, k), lambda i: (i, 0)), pl.BlockSpec((k, wn), lambda i: (0, col))]
    args = [x, w]
    if rope is not None:
        tab_blocks = rope[0].shape[0] // tm
        in_specs += [pl.BlockSpec((tm, LANES), lambda i: (i % tab_blocks, 0))] * 3
        args += list(rope)
    return pl.pallas_call(
        functools.partial(_mm_deint_kernel, dil=dil, rope=rope is not None), grid=(t // tm,),
        in_specs=in_specs, out_specs=pl.BlockSpec((dil, tm // dil, wn), lambda i: (0, i, 0)),
        out_shape=jax.ShapeDtypeStruct((dil, t // dil, wn), BF16),
        scratch_shapes=[pltpu.VMEM((wn // LANES, tm, LANES), F32)] if dil > 1 else [],
        compiler_params=pltpu.CompilerParams(dimension_semantics=("parallel",), vmem_limit_bytes=VMEM_LIMIT),
        name=f"proj_deint{dil}",
    )(*args)


GROUPS_PER_SLAB = LANES // SSM_GROUP


def _mm_ssm_kernel(x_ref, w_ref, xg_ref, z_ref, slab_scr, grp_scr, *, chunk):
    tm = x_ref.shape[0]
    n_chunks = tm // chunk
    x = x_ref[...].astype(BF16)
    z_ref[...] = jnp.dot(x, w_ref[:, D_MODEL:], preferred_element_type=F32).astype(z_ref.dtype)
    u = jnp.dot(x, w_ref[:, :D_MODEL], preferred_element_type=F32)
    for c in range(D_MODEL // LANES):
        slab_scr[c] = u[:, c * LANES:(c + 1) * LANES]
    for s in range(chunk):
        for c in range(D_MODEL // LANES):
            rows = slab_scr[c, pl.ds(s, n_chunks, stride=chunk), :]
            for gg in range(GROUPS_PER_SLAB):
                grp_scr[c * GROUPS_PER_SLAB + gg, :, s * SSM_GROUP:(s + 1) * SSM_GROUP] = (
                    rows[:, gg * SSM_GROUP:(gg + 1) * SSM_GROUP])
    xg_ref[...] = grp_scr[...].astype(xg_ref.dtype)


def _project_ssm(x, w, chunk):
    t, k = x.shape
    tm = _token_tile(t, 512)
    cw = chunk * SSM_GROUP
    return pl.pallas_call(
        functools.partial(_mm_ssm_kernel, chunk=chunk), grid=(t // tm,),
        in_specs=[pl.BlockSpec((tm, k), lambda i: (i, 0)), pl.BlockSpec(w.shape, lambda i: (0, 0))],
        out_specs=[pl.BlockSpec((N_SSM_GROUPS, tm // chunk, cw), lambda i: (0, i, 0)),
                   pl.BlockSpec((tm, D_MODEL), lambda i: (i, 0))],
        out_shape=[jax.ShapeDtypeStruct((N_SSM_GROUPS, t // chunk, cw), BF16),
                   jax.ShapeDtypeStruct((t, D_MODEL), BF16)],
        scratch_shapes=[pltpu.VMEM((D_MODEL // LANES, tm, LANES), F32),
                        pltpu.VMEM((N_SSM_GROUPS, tm // chunk, cw), F32)],
        compiler_params=pltpu.CompilerParams(dimension_semantics=("parallel",), vmem_limit_bytes=VMEM_LIMIT),
        name=f"proj_ssm{chunk}",
    )(x, w)


def _rope_tables(pos, scale):
    half = ROT_DIM // 2
    inv_freq = ROPE_THETA ** (-jnp.arange(0, ROT_DIM, 2, dtype=F32) / ROT_DIM)
    ang = pos[:, None] * inv_freq[None, :]
    cos, sin = jnp.cos(ang), jnp.sin(ang)
    t = pos.shape[0]
    ones = jnp.ones((t, HEAD_DIM - ROT_DIM), F32)
    zeros = jnp.zeros((t, HEAD_DIM - ROT_DIM), F32)
    zhalf = jnp.zeros((t, half), F32)
    cos64 = jnp.concatenate([cos, cos, ones], axis=1)
    sa64 = jnp.concatenate([-sin, zhalf, zeros], axis=1)
    sb64 = jnp.concatenate([zhalf, sin, zeros], axis=1)
    return tuple(scale * jnp.concatenate([a, a], axis=1) for a in (cos64, sa64, sb64))


COEF_BLOCK_STEPS = tuple(SSM_CHUNK << k for k in range(3))
COEF_CARRY_STEPS = tuple(SSM_CHUNK * SUBLANES << k for k in range(6))
COEF_HALF_ROW = len(COEF_BLOCK_STEPS) + len(COEF_CARRY_STEPS)
COEF_IN_BLOCK_ROW = 16
N_COEF_ROWS = COEF_IN_BLOCK_ROW + SUBLANES


def _cpair_mul(a, b):
    return a[0] * b[0] - a[1] * b[1], a[0] * b[1] + a[1] * b[0]


def _abar_and_factor(a_re, a_im, dt):
    mag = jnp.exp(dt * a_re)
    ab = (mag * jnp.cos(dt * a_im), mag * jnp.sin(dt * a_im))
    den = a_re * a_re + a_im * a_im
    nr, ni = ab[0] - 1.0, ab[1]
    fac = ((nr * a_re + ni * a_im) / den, (ni * a_re - nr * a_im) / den)
    return ab, fac


def _powers(ab, n):
    out = [(jnp.ones_like(ab[0]), jnp.zeros_like(ab[0]))]
    for _ in range(n):
        out.append(_cpair_mul(out[-1], ab))
    return out


def _ssm_ops_kernel(arow_ref, acol_ref, bt_ref, ct_ref, d_ref, t_ref, bin_ref, cout_ref, ca_ref, cb_ref,
                    rhs_scr, t_scr):
    c, m, p = SSM_CHUNK, SSM_GROUP, SSM_STATE
    hi = lax.Precision.HIGHEST
    ab, fac = _abar_and_factor(arow_ref[0, 0:1, :], arow_ref[0, 1:2, :], arow_ref[0, 2:3, :])
    bbar = _cpair_mul((bt_ref[0, 0], bt_ref[0, 1]), fac)
    pw = _powers(ab, c)
    for s in range(c):
        blk = _cpair_mul(bbar, pw[c - 1 - s])
        bin_ref[0, s * m:(s + 1) * m, 0:p] = blk[0].astype(bin_ref.dtype)
        bin_ref[0, s * m:(s + 1) * m, p:2 * p] = blk[1].astype(bin_ref.dtype)
    coef = {}
    cur = pw[c]
    for step in COEF_BLOCK_STEPS + COEF_CARRY_STEPS:
        coef[step] = cur
        cur = _cpair_mul(cur, cur)
    rows = [coef[s] for s in COEF_BLOCK_STEPS + COEF_CARRY_STEPS] + [pw[c // 2]]
    rows += [pw[0]] * (COEF_IN_BLOCK_ROW - len(rows))
    in_block = [pw[0]]
    for _ in range(SUBLANES - 1):
        in_block.append(_cpair_mul(in_block[-1], pw[c]))
    rows += in_block
    for k, (re, im) in enumerate(rows):
        ca_ref[0, k:k + 1, 0:p] = re
        ca_ref[0, k:k + 1, p:2 * p] = re
        cb_ref[0, k:k + 1, 0:p] = -im
        cb_ref[0, k:k + 1, p:2 * p] = im

    abc, _ = _abar_and_factor(acol_ref[0, :, 0:1], acol_ref[0, :, 1:2], acol_ref[0, :, 2:3])
    pwc = _powers(abc, c)
    ct = (ct_ref[0, 0], ct_ref[0, 1])
    for t in range(c + 1):
        blk = _cpair_mul(ct, pwc[t])
        if t < c:
            rhs_scr[0, :, t * m:(t + 1) * m] = blk[0]
            rhs_scr[1, :, t * m:(t + 1) * m] = blk[1]
        if t >= 1:
            rhs_scr[2, :, (t - 1) * m:t * m] = blk[0]
            rhs_scr[3, :, (t - 1) * m:t * m] = -blk[1]
    cout_ref[0, 0:p, :] = rhs_scr[2].astype(cout_ref.dtype)
    cout_ref[0, p:2 * p, :] = rhs_scr[3].astype(cout_ref.dtype)

    kern = (jnp.dot(bbar[0], rhs_scr[0], precision=hi, preferred_element_type=F32)
            - jnp.dot(bbar[1], rhs_scr[1], precision=hi, preferred_element_type=F32))
    lane = lax.broadcasted_iota(jnp.int32, kern.shape, 1)
    for s in range(c):
        shifted = kern if s == 0 else jnp.where(lane >= s * m, pltpu.roll(kern, s * m, 1), 0.0)
        t_scr[s * m:(s + 1) * m, :] = shifted
    row = lax.broadcasted_iota(jnp.int32, t_scr.shape, 0)
    col = lax.broadcasted_iota(jnp.int32, t_scr.shape, 1)
    t_ref[0] = (t_scr[...] + jnp.where(row == col, d_ref[0], 0.0)).astype(t_ref.dtype)


def _ssm_operators(a_re, a_im, log_dt, b_re, b_im, c_re, c_im, d_skip):
    g, p, m, c = N_SSM_GROUPS, SSM_STATE, SSM_GROUP, SSM_CHUNK
    dt = jnp.broadcast_to(jnp.exp(log_dt.astype(F32))[:, None], (g, p))
    arow = jnp.stack([a_re.astype(F32), a_im.astype(F32), dt], axis=1)
    acol = arow.transpose(0, 2, 1)
    bt = jnp.stack([b_re, b_im], axis=1).astype(F32).transpose(0, 1, 3, 2)
    ct = jnp.stack([c_re, c_im], axis=1).astype(F32).transpose(0, 1, 3, 2)
    d_rep = jnp.tile(d_skip.astype(F32).reshape(g, 1, m), (1, 1, c))
    cw = c * m

    def spec(*shape):
        return pl.BlockSpec((1,) + shape, lambda i: (i,) + (0,) * len(shape))

    return pl.pallas_call(
        _ssm_ops_kernel, grid=(g,),
        in_specs=[spec(3, p), spec(p, 3), spec(2, m, p), spec(2, p, m), spec(1, cw)],
        out_specs=[spec(cw, cw), spec(cw, 2 * p), spec(2 * p, cw), spec(N_COEF_ROWS, 2 * p), spec(N_COEF_ROWS, 2 * p)],
        out_shape=[jax.ShapeDtypeStruct((g, cw, cw), BF16), jax.ShapeDtypeStruct((g, cw, 2 * p), BF16),
                   jax.ShapeDtypeStruct((g, 2 * p, cw), BF16), jax.ShapeDtypeStruct((g, N_COEF_ROWS, 2 * p), F32),
                   jax.ShapeDtypeStruct((g, N_COEF_ROWS, 2 * p), F32)],
        scratch_shapes=[pltpu.VMEM((4, p, cw), F32), pltpu.VMEM((cw, cw), F32)],
        compiler_params=pltpu.CompilerParams(dimension_semantics=("parallel",)),
        name="s5_operators",
    )(arow, acol, bt, ct, d_rep)


def _swap_halves(x):
    return pltpu.roll(x, SSM_STATE, x.ndim - 1)


def _cmul(x, xs, ca, cb):
    return x * ca + xs * cb


def _ssm_kernel(x_ref, xs_ref, h0_ref, t_ref, bin_ref, cout_ref, ca_ref, cb_ref,
                y_ref, hl_ref, ys_ref, hs_ref, l_scr, e_scr, *, n_seq, rows_per_seq):
    rows = n_seq * rows_per_seq
    n_blocks = rows_per_seq // SUBLANES
    ca = ca_ref[0]
    cb = cb_ref[0]
    x = x_ref[0]
    v = jnp.dot(x, bin_ref[0], preferred_element_type=F32)
    rmod = lax.broadcasted_iota(jnp.int32, (rows, LANES), 0) & (SUBLANES - 1)

    for k, dist in enumerate((1, 2, 4)):
        sh = jnp.where(rmod >= dist, pltpu.roll(v, dist, 0), 0.0)
        v = v + _cmul(sh, _swap_halves(sh), ca[k:k + 1], cb[k:k + 1])
    l_scr[...] = v

    assert n_blocks & (n_blocks - 1) == 0 and n_blocks <= 1 << len(COEF_CARRY_STEPS)
    e = l_scr[pl.ds(SUBLANES - 1, rows // SUBLANES, stride=SUBLANES), :]
    bmod = lax.broadcasted_iota(jnp.int32, e.shape, 0) & (n_blocks - 1)
    for k in range(n_blocks.bit_length() - 1):
        row = len(COEF_BLOCK_STEPS) + k
        sh = jnp.where(bmod >= (1 << k), pltpu.roll(e, 1 << k, 0), 0.0)
        e = e + _cmul(sh, _swap_halves(sh), ca[row:row + 1], cb[row:row + 1])
    hl_ref[0] = jnp.concatenate([e[(s + 1) * n_blocks - 1:(s + 1) * n_blocks] for s in range(n_seq)], axis=0)
    e_prev = jnp.where(bmod >= 1, pltpu.roll(e, 1, 0), 0.0)
    for r in range(SUBLANES):
        e_scr[pl.ds(r, rows // SUBLANES, stride=SUBLANES), :] = e_prev

    lshift = jnp.where(rmod >= 1, pltpu.roll(v, 1, 0), 0.0)
    ebc = e_scr[...]
    pwa = ca[COEF_IN_BLOCK_ROW:COEF_IN_BLOCK_ROW + SUBLANES]
    pwb = cb[COEF_IN_BLOCK_ROW:COEF_IN_BLOCK_ROW + SUBLANES]
    shape3 = (rows // SUBLANES, SUBLANES, LANES)
    h = (lshift.reshape(shape3) + ebc.reshape(shape3) * pwa[None]
         + _swap_halves(ebc).reshape(shape3) * pwb[None]).reshape(rows, LANES)
    y = jnp.dot(x, t_ref[0], preferred_element_type=F32)
    y = y + jnp.dot(h.astype(BF16), cout_ref[0], preferred_element_type=F32)
    y_ref[0] = y.astype(y_ref.dtype)

    half = SSM_CHUNK * SSM_GROUP // 2
    xs = xs_ref[0]
    h0 = h0_ref[0]
    ys = jnp.dot(xs, t_ref[0, :half, :half], preferred_element_type=F32)
    ys = ys + jnp.dot(h0.astype(BF16), cout_ref[0, :, :half], preferred_element_type=F32)
    ys_ref[0] = ys.astype(ys_ref.dtype)
    hs = _cmul(h0, _swap_halves(h0), ca[COEF_HALF_ROW:COEF_HALF_ROW + 1], cb[COEF_HALF_ROW:COEF_HALF_ROW + 1])
    hs_ref[0] = hs + jnp.dot(xs, bin_ref[0, half:, :], preferred_element_type=F32)


def _ssm(xg, xsg, h0g, ops, *, n_seq):
    t_op, b_in, c_out, ca, cb = ops
    g, rows, cw = xg.shape
    n_s = xsg.shape[1]
    hw = xsg.shape[2]

    def spec(shape):
        return pl.BlockSpec((1,) + shape, lambda i: (i, 0, 0))

    return pl.pallas_call(
        functools.partial(_ssm_kernel, n_seq=n_seq, rows_per_seq=rows // n_seq),
        grid=(g,),
        in_specs=[spec((rows, cw)), spec((n_s, hw)), spec((n_s, LANES)), spec((cw, cw)), spec((cw, LANES)),
                  spec((LANES, cw)), spec((N_COEF_ROWS, LANES)), spec((N_COEF_ROWS, LANES))],
        out_specs=[spec((rows, cw)), spec((n_seq, LANES)), spec((n_s, hw)), spec((n_s, LANES))],
        out_shape=[jax.ShapeDtypeStruct((g, rows, cw), BF16), jax.ShapeDtypeStruct((g, n_seq, LANES), F32),
                   jax.ShapeDtypeStruct((g, n_s, hw), BF16), jax.ShapeDtypeStruct((g, n_s, LANES), F32)],
        scratch_shapes=[pltpu.VMEM((rows, LANES), F32)] * 2,
        compiler_params=pltpu.CompilerParams(dimension_semantics=("parallel",), vmem_limit_bytes=VMEM_LIMIT),
        name="s5_scan",
    )(xg, xsg, h0g, t_op, b_in, c_out, ca, cb)


def _post_layer(x, sub, p, g_ref, b_ref, wpg_ref, wpe_ref, alpha):
    r = alpha * x + sub
    mu = jnp.mean(r, axis=-1, keepdims=True)
    cen = r - mu
    var = jnp.mean(cen * cen, axis=-1, keepdims=True)
    h = cen * lax.rsqrt(var + LN_EPS) * g_ref[...] + b_ref[...]
    gate = _sigmoid(jnp.dot(h.astype(BF16), wpg_ref[...], preferred_element_type=F32))
    ple = jnp.dot(p.astype(BF16), wpe_ref[...], preferred_element_type=F32)
    return h + gate * ple


def _glu_post_kernel(yg_ref, z_ref, x_ref, p_ref, g_ref, b_ref, wglu_ref, wout_ref, wpg_ref, wpe_ref,
                     xo_ref, xb_ref, grp_scr, stage_scr, slab_scr, *, alpha, chunk):
    n_chunks = yg_ref.shape[1]
    grp_scr[...] = yg_ref[...].astype(F32)
    for t in range(chunk):
        for c in range(D_MODEL // LANES):
            for gg in range(GROUPS_PER_SLAB):
                stage_scr[:, gg * SSM_GROUP:(gg + 1) * SSM_GROUP] = (
                    grp_scr[c * GROUPS_PER_SLAB + gg, :, t * SSM_GROUP:(t + 1) * SSM_GROUP])
            slab_scr[c, pl.ds(t, n_chunks, stride=chunk), :] = stage_scr[...]
    y = jnp.concatenate([slab_scr[c] for c in range(D_MODEL // LANES)], axis=1)
    g = _gelu_tanh(y)
    t = jnp.dot(g.astype(BF16), wglu_ref[...], preferred_element_type=F32)
    z = z_ref[...].astype(F32)
    glu = g * _sigmoid(t) * (z * _sigmoid(z))
    sub = jnp.dot(glu.astype(BF16), wout_ref[...], preferred_element_type=F32)
    out = _post_layer(x_ref[...], sub, p_ref[...], g_ref, b_ref, wpg_ref, wpe_ref, alpha)
    xo_ref[...] = out
    xb_ref[...] = out.astype(BF16)


def _interleave_rows(ref, scr, dil):
    if dil == 1:
        return ref[0].astype(F32)
    rows = ref.shape[1]
    slabs = []
    for c in range(ref.shape[2] // LANES):
        for r in range(dil):
            scr[c, pl.ds(r, rows, stride=dil), :] = ref[r, :, c * LANES:(c + 1) * LANES].astype(F32)
        slabs.append(scr[c])
    return slabs[0] if len(slabs) == 1 else jnp.concatenate(slabs, axis=1)


def _comb_post_kernel(*refs, dils, alpha):
    n = len(dils)
    if n == 1:
        o_ref = refs[0]
        z_ref, x_ref, p_ref, g_ref, b_ref, wout_ref, wpg_ref, wpe_ref, xo_ref, xb_ref = refs[1:]
        o = o_ref[...].astype(F32)
    else:
        o_refs, lse_refs = refs[:n], refs[n:2 * n]
        z_ref, x_ref, p_ref, g_ref, b_ref, wout_ref, wpg_ref, wpe_ref, xo_ref, xb_ref = refs[2 * n:2 * n + 10]
        scrs = list(refs[2 * n + 10:])
        lses = [_interleave_rows(r, scrs.pop(0) if d > 1 else None, d) for r, d in zip(lse_refs, dils)]
        top = functools.reduce(jnp.maximum, lses)
        ws = [jnp.exp(l - top) for l in lses]
        tot = functools.reduce(lambda a, c: a + c, ws)
        src = lax.broadcasted_iota(jnp.int32, (LANES, ATTN_WIDTH), 0)
        dst = lax.broadcasted_iota(jnp.int32, (LANES, ATTN_WIDTH), 1)
        lanes_per_head = LANES // N_HEADS
        expand = jnp.where(src == (dst >> (HEAD_DIM.bit_length() - 1)) * lanes_per_head, 1.0, 0.0).astype(BF16)
        o = None
        for w, r, d in zip(ws, o_refs, dils):
            a = w / tot
            hi = a.astype(BF16)
            lo = (a - hi.astype(F32)).astype(BF16)
            wide = (jnp.dot(hi, expand, preferred_element_type=F32) + jnp.dot(lo, expand, preferred_element_type=F32))
            term = wide * _interleave_rows(r, scrs.pop(0) if d > 1 else None, d)
            o = term if o is None else o + term
    z = z_ref[...].astype(F32)
    gated = o * (z * _sigmoid(z))
    sub = jnp.dot(gated.astype(BF16), wout_ref[...], preferred_element_type=F32)
    out = _post_layer(x_ref[...], sub, p_ref[...], g_ref, b_ref, wpg_ref, wpe_ref, alpha)
    xo_ref[...] = out
    xb_ref[...] = out.astype(BF16)


def _tok_spec(tm, width, col=0):
    return pl.BlockSpec((tm, width), lambda i: (i, col))


def _full_spec(shape):
    return pl.BlockSpec(shape, lambda i: (0,) * len(shape))


def _post_outputs(t, tm):
    return dict(
        out_specs=[_tok_spec(tm, D_MODEL), _tok_spec(tm, D_MODEL)],
        out_shape=[jax.ShapeDtypeStruct((t, D_MODEL), F32), jax.ShapeDtypeStruct((t, D_MODEL), BF16)],
        compiler_params=pltpu.CompilerParams(dimension_semantics=("parallel",), vmem_limit_bytes=VMEM_LIMIT),
    )


def _glu_post(yg, z, x, p, layer, ln_g, ln_b, w_glu, w_out, w_pg, w_pe, alpha):
    t = x.shape[0]
    tm = _token_tile(t, 512)
    ple = p.shape[2]
    n_grp, _, cw = yg.shape
    chunk = cw // SSM_GROUP
    return pl.pallas_call(
        functools.partial(_glu_post_kernel, alpha=alpha, chunk=chunk), grid=(t // tm,),
        in_specs=[pl.BlockSpec((n_grp, tm // chunk, cw), lambda i: (0, i, 0)), _tok_spec(tm, D_MODEL),
                  _tok_spec(tm, D_MODEL), pl.BlockSpec((None, tm, ple), lambda i: (layer, i, 0)),
                  _full_spec((1, D_MODEL)), _full_spec((1, D_MODEL)), _full_spec((D_MODEL, D_MODEL)),
                  _full_spec((D_MODEL, D_MODEL)), _full_spec((D_MODEL, D_MODEL)), _full_spec((ple, D_MODEL))],
        scratch_shapes=[pltpu.VMEM((n_grp, tm // chunk, cw), F32), pltpu.VMEM((tm // chunk, LANES), F32),
                        pltpu.VMEM((D_MODEL // LANES, tm, LANES), F32)],
        name="glu_post", **_post_outputs(t, tm),
    )(yg, z, x, p, ln_g, ln_b, w_glu, w_out, w_pg, w_pe)


def _comb_post(os_, lses, dils, z, z_col, x, p, layer, ln_g, ln_b, w_out, w_pg, w_pe, alpha):
    t = x.shape[0]
    tm = _token_tile(t, 512)
    ple = p.shape[2]
    if len(dils) == 1:
        group_specs = [_tok_spec(tm, D_MODEL)]
        scratch = []
    else:
        group_specs = [pl.BlockSpec((d, tm // d, D_MODEL), lambda i: (0, i, 0)) for d in dils]
        group_specs += [pl.BlockSpec((d, tm // d, LANES), lambda i: (0, i, 0)) for d in dils]
        scratch = [pltpu.VMEM((1, tm, LANES), F32) for d in dils if d > 1]
        scratch += [pltpu.VMEM((D_MODEL // LANES, tm, LANES), F32) for d in dils if d > 1]
    return pl.pallas_call(
        functools.partial(_comb_post_kernel, dils=tuple(dils), alpha=alpha), grid=(t // tm,),
        in_specs=group_specs
        + [_tok_spec(tm, D_MODEL, z_col), _tok_spec(tm, D_MODEL), pl.BlockSpec((None, tm, ple), lambda i: (layer, i, 0)),
           _full_spec((1, D_MODEL)), _full_spec((1, D_MODEL)), _full_spec((D_MODEL, D_MODEL)),
           _full_spec((D_MODEL, D_MODEL)), _full_spec((ple, D_MODEL))],
        scratch_shapes=scratch, name="attn_out_post", **_post_outputs(t, tm),
    )(*os_, *lses, z, x, p, ln_g, ln_b, w_out, w_pg, w_pe)


def _attn_prompt_kernel(q_ref, kp_ref, kc_ref, vp_ref, vc_ref, o_ref, lse_ref):
    ib = pl.program_id(2)
    tq = q_ref.shape[0]
    row = lax.broadcasted_iota(jnp.int32, (tq, 2 * tq), 0)
    col = lax.broadcasted_iota(jnp.int32, (tq, 2 * tq), 1)
    back = row + tq - col
    valid = (back >= 0) & (back <= WINDOW_KEYS) & ((col >= tq) | (ib > 0))
    lane = lax.broadcasted_iota(jnp.int32, (tq, LANES), 1)
    low = lane < HEAD_DIM
    head_of_lane = lane >> ((LANES // N_HEADS).bit_length() - 1)
    lse_tile = jnp.zeros((tq, LANES), F32)
    nt = (((1,), (1,)), ((), ()))
    for j in range(ATTN_WIDTH // LANES):
        sl = slice(j * LANES, (j + 1) * LANES)
        q2 = q_ref[:, sl]
        k2 = jnp.concatenate([kp_ref[:, sl], kc_ref[:, sl]], axis=0)
        v2 = jnp.concatenate([vp_ref[:, sl], vc_ref[:, sl]], axis=0)
        outs = []
        for sel in (low, jnp.logical_not(low)):
            qm = jnp.where(sel, q2, jnp.zeros_like(q2))
            s = lax.dot_general(qm, k2, nt, preferred_element_type=F32)
            s = jnp.where(valid, s, NEG_INF)
            m = jnp.max(s, axis=1, keepdims=True)
            pr = jnp.exp(s - m)
            l = jnp.sum(pr, axis=1, keepdims=True)
            o = jnp.dot(pr.astype(BF16), v2, preferred_element_type=F32)
            outs.append((o / l, m + jnp.log(l)))
        o_ref[:, sl] = jnp.where(low, outs[0][0], outs[1][0]).astype(o_ref.dtype)
        for hh in range(2):
            lse_tile = jnp.where(head_of_lane == 2 * j + hh, outs[hh][1], lse_tile)
    lse_ref[...] = lse_tile


def _attn_prompt_group(q, k, v, n_seq, seq_len):
    dil, rows, _ = q.shape
    tq = WINDOW_KEYS
    nb = seq_len // dil // tq
    assert rows == n_seq * nb * tq

    def cur(b, r, ib):
        return (r, b * nb + ib, 0)

    def prev(b, r, ib):
        return (r, b * nb + jnp.maximum(ib - 1, 0), 0)

    blk = (None, tq, D_MODEL)
    return pl.pallas_call(
        _attn_prompt_kernel, grid=(n_seq, dil, nb),
        in_specs=[pl.BlockSpec(blk, cur), pl.BlockSpec(blk, prev), pl.BlockSpec(blk, cur),
                  pl.BlockSpec(blk, prev), pl.BlockSpec(blk, cur)],
        out_specs=[pl.BlockSpec(blk, cur), pl.BlockSpec((None, tq, LANES), cur)],
        out_shape=[jax.ShapeDtypeStruct((dil, rows, D_MODEL), BF16), jax.ShapeDtypeStruct((dil, rows, LANES), F32)],
        compiler_params=pltpu.CompilerParams(dimension_semantics=("parallel", "parallel", "arbitrary"),
                                             vmem_limit_bytes=VMEM_LIMIT),
        name=f"attn_prompt_d{dil}",
    )(q, k, k, v, v)


SAMPLE_KEY_CHUNK = 512


def _attn_sample_kernel(q_ref, kvn_ref, c0_ref, c1_ref, c2_ref, o_ref, s_scr, m_scr):
    step = pl.program_id(1)
    ds = q_ref.shape[1]
    n_rows = N_HEADS * ds
    assert ds & (ds - 1) == 0 and n_rows == LANES
    rho = lax.broadcasted_iota(jnp.int32, (n_rows, ATTN_WIDTH), 0)
    lane = lax.broadcasted_iota(jnp.int32, (n_rows, ATTN_WIDTH), 1)
    head_mask = (rho >> (ds.bit_length() - 1)) == (lane >> (HEAD_DIM.bit_length() - 1))
    cache_refs = (c0_ref, c1_ref, c2_ref)
    nt = (((1,), (1,)), ((), ()))
    kvn = kvn_ref[0].astype(F32)

    def new_block(g, off):
        piece = kvn[:, off + g * ATTN_WIDTH: off + (g + 1) * ATTN_WIDTH]
        pad = jnp.zeros((LANES - ds, ATTN_WIDTH), F32)
        return jnp.concatenate([piece, pad], axis=0).astype(BF16)

    chunks = []
    col = 0
    for g, c_ref in enumerate(cache_refs):
        width = c_ref.shape[3]
        for c0 in range(0, width, SAMPLE_KEY_CHUNK):
            cw = min(SAMPLE_KEY_CHUNK, width - c0)
            chunks.append((g, c0, cw, col))
            col += cw
    new_cols = [col + g * LANES for g in range(N_ATT_GROUPS)]

    @pl.when(step == 0)
    def _():
        q_all = q_ref[0].astype(F32)
        q_rows = []
        for g in range(N_ATT_GROUPS):
            qg = q_all[:, g * ATTN_WIDTH:(g + 1) * ATTN_WIDTH]
            q_rows.append(jnp.where(head_mask, jnp.concatenate([qg] * N_HEADS, axis=0), 0.0).astype(BF16))
        m = jnp.full((n_rows, 1), NEG_INF, F32)
        for g, c0, cw, col0 in chunks:
            kt = cache_refs[g][0, 0, :, c0:c0 + cw].astype(BF16)
            s = jnp.dot(q_rows[g], kt, preferred_element_type=F32)
            qi = lax.broadcasted_iota(jnp.int32, (n_rows, cw), 0) & (ds - 1)
            back = lax.broadcasted_iota(jnp.int32, (n_rows, cw), 1) + c0 - qi
            s = jnp.where((back >= 0) & ((back & (DILATIONS[g] - 1)) == 0), s, NEG_INF)
            s_scr[:, col0:col0 + cw] = s
            m = jnp.maximum(m, jnp.max(s, axis=1, keepdims=True))
        qi = lax.broadcasted_iota(jnp.int32, (n_rows, LANES), 0) & (ds - 1)
        kappa = lax.broadcasted_iota(jnp.int32, (n_rows, LANES), 1)
        for g in range(N_ATT_GROUPS):
            s = lax.dot_general(q_rows[g], new_block(g, 0), nt, preferred_element_type=F32)
            ahead = qi - kappa
            s = jnp.where((ahead >= 0) & ((ahead & (DILATIONS[g] - 1)) == 0) & (kappa < ds), s, NEG_INF)
            s_scr[:, new_cols[g]:new_cols[g] + LANES] = s
            m = jnp.maximum(m, jnp.max(s, axis=1, keepdims=True))
        m_scr[...] = jnp.broadcast_to(m, m_scr.shape)

    @pl.when(step == 1)
    def _():
        m = m_scr[:, 0:1]
        l = jnp.zeros((n_rows, 1), F32)
        acc = jnp.zeros((n_rows, ATTN_WIDTH), F32)
        for g, c0, cw, col0 in chunks:
            pr = jnp.exp(s_scr[:, col0:col0 + cw] - m)
            l = l + jnp.sum(pr, axis=1, keepdims=True)
            vt = cache_refs[g][0, 0, :, c0:c0 + cw].astype(BF16)
            acc = acc + lax.dot_general(pr.astype(BF16), vt, nt, preferred_element_type=F32)
        for g in range(N_ATT_GROUPS):
            pr = jnp.exp(s_scr[:, new_cols[g]:new_cols[g] + LANES] - m)
            l = l + jnp.sum(pr, axis=1, keepdims=True)
            acc = acc + jnp.dot(pr.astype(BF16), new_block(g, N_ATT_GROUPS * ATTN_WIDTH),
                                preferred_element_type=F32)
        acc = acc / l
        acc3 = acc.reshape(N_HEADS, ds, ATTN_WIDTH)
        low = lax.broadcasted_iota(jnp.int32, (ds, LANES), 1) < HEAD_DIM
        for c in range(ATTN_WIDTH // LANES):
            sl = slice(c * LANES, (c + 1) * LANES)
            o_ref[0, :, sl] = jnp.where(low, acc3[2 * c][:, sl], acc3[2 * c + 1][:, sl])


def _cache_views(caches):
    views = []
    for cache, dil in zip(caches, DILATIONS):
        n, width = cache.shape[:2]
        assert width == WINDOW_KEYS * dil and cache.shape[2:] == (2, N_HEADS, HEAD_DIM)
        views.append(cache.transpose(0, 2, 3, 4, 1).reshape(n, 2, ATTN_WIDTH, width))
    return views


def _attn_sample(qz_s, kv_s, views, n_seq, ds):
    q3 = qz_s.reshape(n_seq, ds, qz_s.shape[1])
    kv3 = kv_s.reshape(n_seq, ds, kv_s.shape[1])
    n_cols = sum(v.shape[3] for v in views) + N_ATT_GROUPS * LANES

    def tok_spec(width):
        return pl.BlockSpec((1, ds, width), lambda n, s: (n, 0, 0))

    o = pl.pallas_call(
        _attn_sample_kernel, grid=(n_seq, 2),
        in_specs=[tok_spec(q3.shape[2]), tok_spec(kv3.shape[2])]
        + [pl.BlockSpec((1, 1, ATTN_WIDTH, v.shape[3]), lambda n, s: (n, s, 0, 0)) for v in views],
        out_specs=tok_spec(ATTN_WIDTH),
        out_shape=jax.ShapeDtypeStruct((n_seq, ds, ATTN_WIDTH), F32),
        scratch_shapes=[pltpu.VMEM((N_HEADS * ds, n_cols), F32), pltpu.VMEM((N_HEADS * ds, LANES), F32)],
        compiler_params=pltpu.CompilerParams(dimension_semantics=("parallel", "arbitrary"),
                                             vmem_limit_bytes=VMEM_LIMIT),
        name="attn_sample",
    )(q3, kv3, *views)
    return o.reshape(n_seq * ds, ATTN_WIDTH)


def kernel(x_prompt, x_sample, state_ssm_re, state_ssm_im, cache_kv_w128, cache_kv_w512, cache_kv_w2048, p_prompt, p_sample, ln_g, ln_b, w_pe, w_pg, w_in_a, a_re, a_im, log_dt, b_re, b_im, c_re, c_im, d_skip, w_glu, w_out_a, w_kv, w_in_b, w_out_b):
    depth = ln_g.shape[0]
    n_a = w_in_a.shape[0]
    batch, seq, _ = x_prompt.shape
    n_s, ds, _ = x_sample.shape
    past = cache_kv_w2048.shape[1]
    assert ds * 2 == SSM_CHUNK and ds <= DILATIONS[2] and past == WINDOWS[2] and seq % (DILATIONS[2] * WINDOW_KEYS) == 0
    alpha = (2 * depth) ** 0.25
    t_p, t_s = batch * seq, n_s * ds
    caches =(cache_kv_w128, cache_kv_w512, cache_kv_w2048)

    xs = [x_prompt.reshape(t_p, D_MODEL), x_sample.reshape(t_s, D_MODEL)]
    xbs = list(xs)
    ps = [p_prompt.reshape(depth, t_p, -1), p_sample.reshape(depth, t_s, -1)]
    wb = lambda w: w.astype(BF16)
    h_prompt, h_sample = [], []

    for i in range(n_a):
        w_in = wb(w_in_a[i])
        (xg, z_p), (xsg, z_s) = _project_ssm(xbs[0], w_in, SSM_CHUNK), _project_ssm(xbs[1], w_in, ds)
        ops = _ssm_operators(a_re[i], a_im[i], log_dt[i], b_re[i], b_im[i], c_re[i], c_im[i], d_skip[i])
        h0g = jnp.concatenate([state_ssm_re[i], state_ssm_im[i]], axis=-1).astype(F32).transpose(1, 0, 2)
        yg, hl, ysg, hs = _ssm(xg, xsg, h0g, ops, n_seq=batch)
        h_prompt.append(hl.transpose(1, 0, 2))
        h_sample.append(hs.transpose(1, 0, 2))
        for k, (y, z) in enumerate(((yg, z_p), (ysg, z_s))):
            xs[k], xbs[k] = _glu_post(y, z, xs[k], ps[k], i, ln_g[i:i + 1], ln_b[i:i + 1], wb(w_glu[i]),
                                      wb(w_out_a[i]), wb(w_pg[i]), wb(w_pe[i]), alpha)

    pos = [jnp.arange(seq, dtype=F32), jnp.tile(past + jnp.arange(ds, dtype=F32), n_s)]
    rope_k = [_rope_tables(p_, 1.0) for p_ in pos]
    rope_q = [tuple(HEAD_DIM ** -0.5 * tab for tab in tabs) for tabs in rope_k]
    q_width = N_ATT_GROUPS * ATTN_WIDTH
    n_grp = N_ATT_GROUPS
    w_kv_b = wb(w_kv)
    k_p = [_project_deint(xbs[0], w_kv_b, grp, DILATIONS[grp], rope_k[0]) for grp in range(n_grp)]
    v_p = [_project_deint(xbs[0], w_kv_b, n_grp + grp, DILATIONS[grp]) for grp in range(n_grp)]
    kv_s = _project(xbs[1], w_kv_b, rope=rope_k[1], rot_cols=q_width, tn=512)
    views = _cache_views(caches)

    for i in range(n_a, depth):
        j = i - n_a
        w_in = wb(w_in_b[j])
        post_w = (ln_g[i:i + 1], ln_b[i:i + 1], wb(w_out_b[j]), wb(w_pg[i]), wb(w_pe[i]), alpha)
        q_p = [_project_deint(xbs[0], w_in, grp, DILATIONS[grp], rope_q[0]) for grp in range(n_grp)]
        z_p = _project_deint(xbs[0], w_in, n_grp, 1).reshape(t_p, D_MODEL)
        parts = [_attn_prompt_group(q_p[grp], k_p[grp], v_p[grp], batch, seq) for grp in range(n_grp)]
        qz_s = _project(xbs[1], w_in, rope=rope_q[1], rot_cols=q_width, tn=512)
        o_s = _attn_sample(qz_s, kv_s, views, n_s, ds)
        xs[0], xbs[0] = _comb_post([pt[0] for pt in parts], [pt[1] for pt in parts], DILATIONS, z_p, 0,
                                   xs[0], ps[0], i, *post_w)
        xs[1], xbs[1] = _comb_post([o_s], [], (1,), qz_s, n_grp, xs[1], ps[1], i, *post_w)

    y_prompt = xs[0].reshape(batch, seq, D_MODEL)
    y_sample = xs[1].reshape(n_s, ds, D_MODEL)
    hp = jnp.stack(h_prompt)
    hsm = jnp.stack(h_sample)

    def prompt_window(arr, dil, width):
        per = arr.reshape(dil, batch, seq // dil, ATTN_WIDTH)[:, :, (seq - width) // dil:]
        return per.transpose(1, 2, 0, 3).reshape(batch, width, N_HEADS, HEAD_DIM).astype(F32)

    kv_prompt = [jnp.stack([prompt_window(k_p[grp], DILATIONS[grp], min(w, seq)),
                            prompt_window(v_p[grp], DILATIONS[grp], min(w, seq))], axis=2)
                 for grp, w in enumerate(WINDOWS)]
    kv_s6 = kv_s.astype(F32).reshape(n_s, ds, 2, N_ATT_GROUPS, N_HEADS, HEAD_DIM)
    kv_sample = [kv_s6[:, :, :, grp] for grp in range(N_ATT_GROUPS)]
    return (y_prompt, y_sample, hp[..., :SSM_STATE], hp[..., SSM_STATE:], hsm[..., :SSM_STATE], hsm[..., SSM_STATE:],
            *kv_prompt, *kv_sample)
```

```python
import functools
import math

import jax
import jax.numpy as jnp
from jax import lax
from jax.experimental import pallas as pl
from jax.experimental.pallas import tpu as pltpu

F32 = jnp.float32
BF16 = jnp.bfloat16

D_MODEL = 1024
N_SSM_GROUPS = 64
SSM_GROUP = 16
SSM_STATE = 64
SSM_CHUNK = 16
N_ATT_GROUPS = 3
WINDOWS = (128, 512, 2048)
DILATIONS = (1, 4, 16)
HEAD_DIM = 64
N_HEADS = 16
ATTN_WIDTH = N_HEADS * HEAD_DIM
ROT_DIM = HEAD_DIM // 4
ROPE_THETA = 500000.0
WINDOW_KEYS = 128
LN_EPS = 1e-5
NEG_INF = -1e30
LANES = 128
SUBLANES = 8
VMEM_LIMIT = 52 * 1024 * 1024


def _sigmoid(x):
    return 1.0 / (1.0 + jnp.exp(-x))


def _gelu_tanh(x):
    return 0.5 * x * (1.0 + jnp.tanh(math.sqrt(2.0 / math.pi) * (x + 0.044715 * (x * x * x))))


def _mm_kernel(x_ref, w_ref, o_ref):
    o_ref[...] = jnp.dot(x_ref[...], w_ref[...], preferred_element_type=F32).astype(o_ref.dtype)


def _mm_rope_kernel(x_ref, w_ref, cos_ref, sa_ref, sb_ref, o_ref, *, rot_blocks):
    j = pl.program_id(1)
    acc = jnp.dot(x_ref[...], w_ref[...], preferred_element_type=F32)

    @pl.when(j < rot_blocks)
    def _():
        o_ref[...] = _rope(acc, cos_ref[...], sa_ref[...], sb_ref[...]).astype(o_ref.dtype)

    @pl.when(j >= rot_blocks)
    def _():
        o_ref[...] = acc.astype(o_ref.dtype)


def _token_tile(n_tokens, want):
    tm = min(want, n_tokens)
    assert n_tokens % tm == 0
    return tm


def _project(x, w, *, rope=None, rot_cols=0, tn=512, out_dtype=BF16):
    t, k = x.shape
    n = w.shape[1]
    tm = _token_tile(t, 1024)
    grid = (t // tm, n // tn)
    x_spec = pl.BlockSpec((tm, k), lambda i, j: (i, 0))
    w_spec = pl.BlockSpec((k, tn), lambda i, j: (0, j))
    o_spec = pl.BlockSpec((tm, tn), lambda i, j: (i, j))
    params = pltpu.CompilerParams(dimension_semantics=("parallel", "arbitrary"), vmem_limit_bytes=VMEM_LIMIT)
    if rope is None:
        return pl.pallas_call(
            _mm_kernel, grid=grid, in_specs=[x_spec, w_spec], out_specs=o_spec,
            out_shape=jax.ShapeDtypeStruct((t, n), out_dtype), compiler_params=params, name="proj",
        )(x, w)
    assert rot_cols % tn == 0
    tab_spec = pl.BlockSpec((tm, LANES), lambda i, j: (i, 0))
    return pl.pallas_call(
        functools.partial(_mm_rope_kernel, rot_blocks=rot_cols // tn), grid=grid,
        in_specs=[x_spec, w_spec, tab_spec, tab_spec, tab_spec], out_specs=o_spec,
        out_shape=jax.ShapeDtypeStruct((t, n), out_dtype), compiler_params=params, name="proj_rope",
    )(x, w, *rope)


PROJ_SUB_ROWS = 512


def _rope(acc, cos, sa, sb):
    half = ROT_DIM // 2
    outs = []
    for c in range(acc.shape[1] // LANES):
        a = acc[:, c * LANES:(c + 1) * LANES]
        outs.append(a * cos + pltpu.roll(a, LANES - half, 1) * sa + pltpu.roll(a, half, 1) * sb)
    return jnp.concatenate(outs, axis=1)


def _mm_deint_kernel(*refs, dil, rope):
    if rope:
        x_ref, w_ref, cos_ref, sa_ref, sb_ref, o_ref = refs[:6]
    else:
        x_ref, w_ref, o_ref = refs[:3]
    tm = x_ref.shape[0]
    sub = min(tm, PROJ_SUB_ROWS)
    for b in range(tm // sub):
        rs = slice(b * sub, (b + 1) * sub)
        acc = jnp.dot(x_ref[rs, :], w_ref[...], preferred_element_type=F32)
        if rope:
            acc = _rope(acc, cos_ref[rs, :], sa_ref[rs, :], sb_ref[rs, :])
        if dil == 1:
            o_ref[0, rs, :] = acc.astype(o_ref.dtype)
            continue
        acc_scr = refs[-1]
        rows = sub // dil
        for c in range(acc.shape[1] // LANES):
            sl = slice(c * LANES, (c + 1) * LANES)
            acc_scr[c, rs, :] = acc[:, sl]
            for r in range(dil):
                o_ref[r, b * rows:(b + 1) * rows, sl] = (
                    acc_scr[c, pl.ds(b * sub + r, rows, stride=dil), :].astype(o_ref.dtype))


def _project_deint(x, w, col, dil, rope=None):
    t, k = x.shape
    tm = _token_tile(t, 1024)
    wn = D_MODEL
    in_specs = [pl.BlockSpec((tm, k), lambda i: (i, 0)), pl.BlockSpec((k, wn), lambda i: (0, col))]
    args = [x, w]
    if rope is not None:
        tab_blocks = rope[0].shape[0] // tm
        in_specs += [pl.BlockSpec((tm, LANES), lambda i: (i % tab_blocks, 0))] * 3
        args += list(rope)
    return pl.pallas_call(
        functools.partial(_mm_deint_kernel, dil=dil, rope=rope is not None), grid=(t // tm,),
        in_specs=in_specs, out_specs=pl.BlockSpec((dil, tm // dil, wn), lambda i: (0, i, 0)),
        out_shape=jax.ShapeDtypeStruct((dil, t // dil, wn), BF16),
        scratch_shapes=[pltpu.VMEM((wn // LANES, tm, LANES), F32)] if dil > 1 else [],
        compiler_params=pltpu.CompilerParams(dimension_semantics=("parallel",), vmem_limit_bytes=VMEM_LIMIT),
        name=f"proj_deint{dil}",
    )(*args)


GROUPS_PER_SLAB = LANES // SSM_GROUP


def _mm_ssm_kernel(x_ref, w_ref, xg_ref, z_ref, slab_scr, grp_scr, *, chunk):
    tm = x_ref.shape[0]
    n_chunks = tm // chunk
    x = x_ref[...].astype(BF16)
    z_ref[...] = jnp.dot(x, w_ref[:, D_MODEL:], preferred_element_type=F32).astype(z_ref.dtype)
    u = jnp.dot(x, w_ref[:, :D_MODEL], preferred_element_type=F32)
    for c in range(D_MODEL // LANES):
        slab_scr[c] = u[:, c * LANES:(c + 1) * LANES]
    for s in range(chunk):
        for c in range(D_MODEL // LANES):
            rows = slab_scr[c, pl.ds(s, n_chunks, stride=chunk), :]
            for gg in range(GROUPS_PER_SLAB):
                grp_scr[c * GROUPS_PER_SLAB + gg, :, s * SSM_GROUP:(s + 1) * SSM_GROUP] = (
                    rows[:, gg * SSM_GROUP:(gg + 1) * SSM_GROUP])
    xg_ref[...] = grp_scr[...].astype(xg_ref.dtype)


def _project_ssm(x, w, chunk):
    t, k = x.shape
    tm = _token_tile(t, 512)
    cw = chunk * SSM_GROUP
    return pl.pallas_call(
        functools.partial(_mm_ssm_kernel, chunk=chunk), grid=(t // tm,),
        in_specs=[pl.BlockSpec((tm, k), lambda i: (i, 0)), pl.BlockSpec(w.shape, lambda i: (0, 0))],
        out_specs=[pl.BlockSpec((N_SSM_GROUPS, tm // chunk, cw), lambda i: (0, i, 0)),
                   pl.BlockSpec((tm, D_MODEL), lambda i: (i, 0))],
        out_shape=[jax.ShapeDtypeStruct((N_SSM_GROUPS, t // chunk, cw), BF16),
                   jax.ShapeDtypeStruct((t, D_MODEL), BF16)],
        scratch_shapes=[pltpu.VMEM((D_MODEL // LANES, tm, LANES), F32),
                        pltpu.VMEM((N_SSM_GROUPS, tm // chunk, cw), F32)],
        compiler_params=pltpu.CompilerParams(dimension_semantics=("parallel",), vmem_limit_bytes=VMEM_LIMIT),
        name=f"proj_ssm{chunk}",
    )(x, w)


def _rope_tables(pos, scale):
    half = ROT_DIM // 2
    inv_freq = ROPE_THETA ** (-jnp.arange(0, ROT_DIM, 2, dtype=F32) / ROT_DIM)
    ang = pos[:, None] * inv_freq[None, :]
    cos, sin = jnp.cos(ang), jnp.sin(ang)
    t = pos.shape[0]
    ones = jnp.ones((t, HEAD_DIM - ROT_DIM), F32)
    zeros = jnp.zeros((t, HEAD_DIM - ROT_DIM), F32)
    zhalf = jnp.zeros((t, half), F32)
    cos64 = jnp.concatenate([cos, cos, ones], axis=1)
    sa64 = jnp.concatenate([-sin, zhalf, zeros], axis=1)
    sb64 = jnp.concatenate([zhalf, sin, zeros], axis=1)
    return tuple(scale * jnp.concatenate([a, a], axis=1) for a in (cos64, sa64, sb64))


COEF_BLOCK_STEPS = tuple(SSM_CHUNK << k for k in range(3))
COEF_CARRY_STEPS = tuple(SSM_CHUNK * SUBLANES << k for k in range(6))
COEF_HALF_ROW = len(COEF_BLOCK_STEPS) + len(COEF_CARRY_STEPS)
COEF_IN_BLOCK_ROW = 16
N_COEF_ROWS = COEF_IN_BLOCK_ROW + SUBLANES


def _cpair_mul(a, b):
    return a[0] * b[0] - a[1] * b[1], a[0] * b[1] + a[1] * b[0]


def _abar_and_factor(a_re, a_im, dt):
    mag = jnp.exp(dt * a_re)
    ab = (mag * jnp.cos(dt * a_im), mag * jnp.sin(dt * a_im))
    den = a_re * a_re + a_im * a_im
    nr, ni = ab[0] - 1.0, ab[1]
    fac = ((nr * a_re + ni * a_im) / den, (ni * a_re - nr * a_im) / den)
    return ab, fac


def _powers(ab, n):
    out = [(jnp.ones_like(ab[0]), jnp.zeros_like(ab[0]))]
    for _ in range(n):
        out.append(_cpair_mul(out[-1], ab))
    return out


OPS_GROUPS_PER_STEP = 8


def _ssm_ops_kernel(*refs):
    io_refs, (rhs_scr, t_scr) = refs[:-2], refs[-2:]
    for j in range(OPS_GROUPS_PER_STEP):
        _ssm_ops_group(*[r.at[pl.ds(j, 1)] for r in io_refs], rhs_scr.at[j], t_scr.at[j])


def _ssm_ops_group(arow_ref, acol_ref, bt_ref, ct_ref, d_ref, t_ref, bin_ref, cout_ref, ca_ref, cb_ref,
                   rhs_scr, t_scr):
    c, m, p = SSM_CHUNK, SSM_GROUP, SSM_STATE
    hi = lax.Precision.HIGHEST
    ab, fac = _abar_and_factor(arow_ref[0, 0:1, :], arow_ref[0, 1:2, :], arow_ref[0, 2:3, :])
    bbar = _cpair_mul((bt_ref[0, 0], bt_ref[0, 1]), fac)
    pw = _powers(ab, c)
    for s in range(c):
        blk = _cpair_mul(bbar, pw[c - 1 - s])
        bin_ref[0, s * m:(s + 1) * m, 0:p] = blk[0].astype(bin_ref.dtype)
        bin_ref[0, s * m:(s + 1) * m, p:2 * p] = blk[1].astype(bin_ref.dtype)
    coef = {}
    cur = pw[c]
    for step in COEF_BLOCK_STEPS + COEF_CARRY_STEPS:
        coef[step] = cur
        cur = _cpair_mul(cur, cur)
    rows = [coef[s] for s in COEF_BLOCK_STEPS + COEF_CARRY_STEPS] + [pw[c // 2]]
    rows += [pw[0]] * (COEF_IN_BLOCK_ROW - len(rows))
    in_block = [pw[0]]
    for _ in range(SUBLANES - 1):
        in_block.append(_cpair_mul(in_block[-1], pw[c]))
    rows += in_block
    for k, (re, im) in enumerate(rows):
        ca_ref[0, k:k + 1, 0:p] = re
        ca_ref[0, k:k + 1, p:2 * p] = re
        cb_ref[0, k:k + 1, 0:p] = -im
        cb_ref[0, k:k + 1, p:2 * p] = im

    abc, _ = _abar_and_factor(acol_ref[0, :, 0:1], acol_ref[0, :, 1:2], acol_ref[0, :, 2:3])
    pwc = _powers(abc, c)
    ct = (ct_ref[0, 0], ct_ref[0, 1])
    for t in range(c + 1):
        blk = _cpair_mul(ct, pwc[t])
        if t < c:
            rhs_scr[0, :, t * m:(t + 1) * m] = blk[0]
            rhs_scr[1, :, t * m:(t + 1) * m] = blk[1]
        if t >= 1:
            rhs_scr[2, :, (t - 1) * m:t * m] = blk[0]
            rhs_scr[3, :, (t - 1) * m:t * m] = -blk[1]
    cout_ref[0, 0:p, :] = rhs_scr[2].astype(cout_ref.dtype)
    cout_ref[0, p:2 * p, :] = rhs_scr[3].astype(cout_ref.dtype)

    kern = (jnp.dot(bbar[0], rhs_scr[0], precision=hi, preferred_element_type=F32)
            - jnp.dot(bbar[1], rhs_scr[1], precision=hi, preferred_element_type=F32))
    lane = lax.broadcasted_iota(jnp.int32, kern.shape, 1)
    for s in range(c):
        shifted = kern if s == 0 else jnp.where(lane >= s * m, pltpu.roll(kern, s * m, 1), 0.0)
        t_scr[s * m:(s + 1) * m, :] = shifted
    row = lax.broadcasted_iota(jnp.int32, t_scr.shape, 0)
    col = lax.broadcasted_iota(jnp.int32, t_scr.shape, 1)
    t_ref[0] = (t_scr[...] + jnp.where(row == col, d_ref[0], 0.0)).astype(t_ref.dtype)


def _ssm_operators(a_re, a_im, log_dt, b_re, b_im, c_re, c_im, d_skip):
    g, p, m, c = N_SSM_GROUPS, SSM_STATE, SSM_GROUP, SSM_CHUNK
    dt = jnp.broadcast_to(jnp.exp(log_dt.astype(F32))[:, None], (g, p))
    arow = jnp.stack([a_re.astype(F32), a_im.astype(F32), dt], axis=1)
    acol = arow.transpose(0, 2, 1)
    bt = jnp.stack([b_re, b_im], axis=1).astype(F32).transpose(0, 1, 3, 2)
    ct = jnp.stack([c_re, c_im], axis=1).astype(F32).transpose(0, 1, 3, 2)
    d_rep = jnp.tile(d_skip.astype(F32).reshape(g, 1, m), (1, 1, c))
    cw = c * m

    gps = OPS_GROUPS_PER_STEP

    def spec(*shape):
        return pl.BlockSpec((gps,) + shape, lambda i: (i,) + (0,) * len(shape))

    return pl.pallas_call(
        _ssm_ops_kernel, grid=(g // gps,),
        in_specs=[spec(3, p), spec(p, 3), spec(2, m, p), spec(2, p, m), spec(1, cw)],
        out_specs=[spec(cw, cw), spec(cw, 2 * p), spec(2 * p, cw), spec(N_COEF_ROWS, 2 * p), spec(N_COEF_ROWS, 2 * p)],
        out_shape=[jax.ShapeDtypeStruct((g, cw, cw), BF16), jax.ShapeDtypeStruct((g, cw, 2 * p), BF16),
                   jax.ShapeDtypeStruct((g, 2 * p, cw), BF16), jax.ShapeDtypeStruct((g, N_COEF_ROWS, 2 * p), F32),
                   jax.ShapeDtypeStruct((g, N_COEF_ROWS, 2 * p), F32)],
        scratch_shapes=[pltpu.VMEM((gps, 4, p, cw), F32), pltpu.VMEM((gps, cw, cw), F32)],
        compiler_params=pltpu.CompilerParams(dimension_semantics=("parallel",)),
        name="s5_operators",
    )(arow, acol, bt, ct, d_rep)


def _swap_halves(x):
    return pltpu.roll(x, SSM_STATE, x.ndim - 1)


def _cmul(x, xs, ca, cb):
    return x * ca + xs * cb


def _ssm_kernel(x_ref, xs_ref, h0_ref, t_ref, bin_ref, cout_ref, ca_ref, cb_ref,
                y_ref, hl_ref, ys_ref, hs_ref, l_scr, e_scr, *, n_seq, rows_per_seq):
    rows = n_seq * rows_per_seq
    n_blocks = rows_per_seq // SUBLANES
    ca = ca_ref[0]
    cb = cb_ref[0]
    x = x_ref[0]
    v = jnp.dot(x, bin_ref[0], preferred_element_type=F32)
    rmod = lax.broadcasted_iota(jnp.int32, (rows, LANES), 0) & (SUBLANES - 1)

    for k, dist in enumerate((1, 2, 4)):
        sh = jnp.where(rmod >= dist, pltpu.roll(v, dist, 0), 0.0)
        v = v + _cmul(sh, _swap_halves(sh), ca[k:k + 1], cb[k:k + 1])
    l_scr[...] = v

    assert n_blocks & (n_blocks - 1) == 0 and n_blocks <= 1 << len(COEF_CARRY_STEPS)
    e = l_scr[pl.ds(SUBLANES - 1, rows // SUBLANES, stride=SUBLANES), :]
    bmod = lax.broadcasted_iota(jnp.int32, e.shape, 0) & (n_blocks - 1)
    for k in range(n_blocks.bit_length() - 1):
        row = len(COEF_BLOCK_STEPS) + k
        sh = jnp.where(bmod >= (1 << k), pltpu.roll(e, 1 << k, 0), 0.0)
        e = e + _cmul(sh, _swap_halves(sh), ca[row:row + 1], cb[row:row + 1])
    hl_ref[0] = jnp.concatenate([e[(s + 1) * n_blocks - 1:(s + 1) * n_blocks] for s in range(n_seq)], axis=0)
    e_prev = jnp.where(bmod >= 1, pltpu.roll(e, 1, 0), 0.0)
    for r in range(SUBLANES):
        e_scr[pl.ds(r, rows // SUBLANES, stride=SUBLANES), :] = e_prev

    lshift = jnp.where(rmod >= 1, pltpu.roll(v, 1, 0), 0.0)
    ebc = e_scr[...]
    pwa = ca[COEF_IN_BLOCK_ROW:COEF_IN_BLOCK_ROW + SUBLANES]
    pwb = cb[COEF_IN_BLOCK_ROW:COEF_IN_BLOCK_ROW + SUBLANES]
    shape3 = (rows // SUBLANES, SUBLANES, LANES)
    h = (lshift.reshape(shape3) + ebc.reshape(shape3) * pwa[None]
         + _swap_halves(ebc).reshape(shape3) * pwb[None]).reshape(rows, LANES)
    y = jnp.dot(x, t_ref[0], preferred_element_type=F32)
    y = y + jnp.dot(h.astype(BF16), cout_ref[0], preferred_element_type=F32)
    y_ref[0] = y.astype(y_ref.dtype)

    half = SSM_CHUNK * SSM_GROUP // 2
    xs = xs_ref[0]
    h0 = h0_ref[0]
    ys = jnp.dot(xs, t_ref[0, :half, :half], preferred_element_type=F32)
    ys = ys + jnp.dot(h0.astype(BF16), cout_ref[0, :, :half], preferred_element_type=F32)
    ys_ref[0] = ys.astype(ys_ref.dtype)
    hs = _cmul(h0, _swap_halves(h0), ca[COEF_HALF_ROW:COEF_HALF_ROW + 1], cb[COEF_HALF_ROW:COEF_HALF_ROW + 1])
    hs_ref[0] = hs + jnp.dot(xs, bin_ref[0, half:, :], preferred_element_type=F32)


def _ssm(xg, xsg, h0g, ops, *, n_seq):
    t_op, b_in, c_out, ca, cb = ops
    g, rows, cw = xg.shape
    n_s = xsg.shape[1]
    hw = xsg.shape[2]

    def spec(shape):
        return pl.BlockSpec((1,) + shape, lambda i: (i, 0, 0))

    return pl.pallas_call(
        functools.partial(_ssm_kernel, n_seq=n_seq, rows_per_seq=rows // n_seq),
        grid=(g,),
        in_specs=[spec((rows, cw)), spec((n_s, hw)), spec((n_s, LANES)), spec((cw, cw)), spec((cw, LANES)),
                  spec((LANES, cw)), spec((N_COEF_ROWS, LANES)), spec((N_COEF_ROWS, LANES))],
        out_specs=[spec((rows, cw)), spec((n_seq, LANES)), spec((n_s, hw)), spec((n_s, LANES))],
        out_shape=[jax.ShapeDtypeStruct((g, rows, cw), BF16), jax.ShapeDtypeStruct((g, n_seq, LANES), F32),
                   jax.ShapeDtypeStruct((g, n_s, hw), BF16), jax.ShapeDtypeStruct((g, n_s, LANES), F32)],
        scratch_shapes=[pltpu.VMEM((rows, LANES), F32)] * 2,
        compiler_params=pltpu.CompilerParams(dimension_semantics=("parallel",), vmem_limit_bytes=VMEM_LIMIT),
        name="s5_scan",
    )(xg, xsg, h0g, t_op, b_in, c_out, ca, cb)


def _post_layer(x, sub, p, g_ref, b_ref, wpg_ref, wpe_ref, alpha):
    r = alpha * x + sub
    mu = jnp.mean(r, axis=-1, keepdims=True)
    cen = r - mu
    var = jnp.mean(cen * cen, axis=-1, keepdims=True)
    h = cen * lax.rsqrt(var + LN_EPS) * g_ref[...] + b_ref[...]
    gate = _sigmoid(jnp.dot(h.astype(BF16), wpg_ref[...], preferred_element_type=F32))
    ple = jnp.dot(p.astype(BF16), wpe_ref[...], preferred_element_type=F32)
    return h + gate * ple


def _glu_post_kernel(yg_ref, z_ref, x_ref, p_ref, g_ref, b_ref, wglu_ref, wout_ref, wpg_ref, wpe_ref,
                     xo_ref, xb_ref, grp_scr, stage_scr, slab_scr, *, alpha, chunk):
    n_chunks = yg_ref.shape[1]
    grp_scr[...] = yg_ref[...].astype(F32)
    for t in range(chunk):
        for c in range(D_MODEL // LANES):
            for gg in range(GROUPS_PER_SLAB):
                stage_scr[:, gg * SSM_GROUP:(gg + 1) * SSM_GROUP] = (
                    grp_scr[c * GROUPS_PER_SLAB + gg, :, t * SSM_GROUP:(t + 1) * SSM_GROUP])
            slab_scr[c, pl.ds(t, n_chunks, stride=chunk), :] = stage_scr[...]
    y = jnp.concatenate([slab_scr[c] for c in range(D_MODEL // LANES)], axis=1)
    g = _gelu_tanh(y)
    t = jnp.dot(g.astype(BF16), wglu_ref[...], preferred_element_type=F32)
    z = z_ref[...].astype(F32)
    glu = g * _sigmoid(t) * (z * _sigmoid(z))
    sub = jnp.dot(glu.astype(BF16), wout_ref[...], preferred_element_type=F32)
    out = _post_layer(x_ref[...], sub, p_ref[...], g_ref, b_ref, wpg_ref, wpe_ref, alpha)
    xo_ref[...] = out
    xb_ref[...] = out.astype(BF16)


def _interleave_rows(ref, scr, dil):
    if dil == 1:
        return ref[0].astype(F32)
    rows = ref.shape[1]
    slabs = []
    for c in range(ref.shape[2] // LANES):
        for r in range(dil):
            scr[c, pl.ds(r, rows, stride=dil), :] = ref[r, :, c * LANES:(c + 1) * LANES].astype(F32)
        slabs.append(scr[c])
    return slabs[0] if len(slabs) == 1 else jnp.concatenate(slabs, axis=1)


def _comb_post_kernel(*refs, dils, alpha):
    n = len(dils)
    if n == 1:
        o_ref = refs[0]
        z_ref, x_ref, p_ref, g_ref, b_ref, wout_ref, wpg_ref, wpe_ref, xo_ref, xb_ref = refs[1:]
        o = o_ref[...].astype(F32)
    else:
        o_refs, lse_refs = refs[:n], refs[n:2 * n]
        z_ref, x_ref, p_ref, g_ref, b_ref, wout_ref, wpg_ref, wpe_ref, xo_ref, xb_ref = refs[2 * n:2 * n + 10]
        scrs = list(refs[2 * n + 10:])
        lses = [_interleave_rows(r, scrs.pop(0) if d > 1 else None, d) for r, d in zip(lse_refs, dils)]
        top = functools.reduce(jnp.maximum, lses)
        ws = [jnp.exp(l - top) for l in lses]
        tot = functools.reduce(lambda a, c: a + c, ws)
        src = lax.broadcasted_iota(jnp.int32, (LANES, ATTN_WIDTH), 0)
        dst = lax.broadcasted_iota(jnp.int32, (LANES, ATTN_WIDTH), 1)
        lanes_per_head = LANES // N_HEADS
        expand = jnp.where(src == (dst >> (HEAD_DIM.bit_length() - 1)) * lanes_per_head, 1.0, 0.0).astype(BF16)
        o = None
        for w, r, d in zip(ws, o_refs, dils):
            a = w / tot
            hi = a.astype(BF16)
            lo = (a - hi.astype(F32)).astype(BF16)
            wide = (jnp.dot(hi, expand, preferred_element_type=F32) + jnp.dot(lo, expand, preferred_element_type=F32))
            term = wide * _interleave_rows(r, scrs.pop(0) if d > 1 else None, d)
            o = term if o is None else o + term
    z = z_ref[...].astype(F32)
    gated = o * (z * _sigmoid(z))
    sub = jnp.dot(gated.astype(BF16), wout_ref[...], preferred_element_type=F32)
    out = _post_layer(x_ref[...], sub, p_ref[...], g_ref, b_ref, wpg_ref, wpe_ref, alpha)
    xo_ref[...] = out
    xb_ref[...] = out.astype(BF16)


def _tok_spec(tm, width, col=0):
    return pl.BlockSpec((tm, width), lambda i: (i, col))


def _full_spec(shape):
    return pl.BlockSpec(shape, lambda i: (0,) * len(shape))


def _post_outputs(t, tm):
    return dict(
        out_specs=[_tok_spec(tm, D_MODEL), _tok_spec(tm, D_MODEL)],
        out_shape=[jax.ShapeDtypeStruct((t, D_MODEL), F32), jax.ShapeDtypeStruct((t, D_MODEL), BF16)],
        compiler_params=pltpu.CompilerParams(dimension_semantics=("parallel",), vmem_limit_bytes=VMEM_LIMIT),
    )


def _glu_post(yg, z, x, p, layer, ln_g, ln_b, w_glu, w_out, w_pg, w_pe, alpha):
    t = x.shape[0]
    tm = _token_tile(t, 512)
    ple = p.shape[2]
    n_grp, _, cw = yg.shape
    chunk = cw // SSM_GROUP
    return pl.pallas_call(
        functools.partial(_glu_post_kernel, alpha=alpha, chunk=chunk), grid=(t // tm,),
        in_specs=[pl.BlockSpec((n_grp, tm // chunk, cw), lambda i: (0, i, 0)), _tok_spec(tm, D_MODEL),
                  _tok_spec(tm, D_MODEL), pl.BlockSpec((None, tm, ple), lambda i: (layer, i, 0)),
                  _full_spec((1, D_MODEL)), _full_spec((1, D_MODEL)), _full_spec((D_MODEL, D_MODEL)),
                  _full_spec((D_MODEL, D_MODEL)), _full_spec((D_MODEL, D_MODEL)), _full_spec((ple, D_MODEL))],
        scratch_shapes=[pltpu.VMEM((n_grp, tm // chunk, cw), F32), pltpu.VMEM((tm // chunk, LANES), F32),
                        pltpu.VMEM((D_MODEL // LANES, tm, LANES), F32)],
        name="glu_post", **_post_outputs(t, tm),
    )(yg, z, x, p, ln_g, ln_b, w_glu, w_out, w_pg, w_pe)


def _comb_post(os_, lses, dils, z, z_col, x, p, layer, ln_g, ln_b, w_out, w_pg, w_pe, alpha):
    t = x.shape[0]
    tm = _token_tile(t, 512)
    ple = p.shape[2]
    if len(dils) == 1:
        group_specs = [_tok_spec(tm, D_MODEL)]
        scratch = []
    else:
        group_specs = [pl.BlockSpec((d, tm // d, D_MODEL), lambda i: (0, i, 0)) for d in dils]
        group_specs += [pl.BlockSpec((d, tm // d, LANES), lambda i: (0, i, 0)) for d in dils]
        scratch = [pltpu.VMEM((1, tm, LANES), F32) for d in dils if d > 1]
        scratch += [pltpu.VMEM((D_MODEL // LANES, tm, LANES), F32) for d in dils if d > 1]
    return pl.pallas_call(
        functools.partial(_comb_post_kernel, dils=tuple(dils), alpha=alpha), grid=(t // tm,),
        in_specs=group_specs
        + [_tok_spec(tm, D_MODEL, z_col), _tok_spec(tm, D_MODEL), pl.BlockSpec((None, tm, ple), lambda i: (layer, i, 0)),
           _full_spec((1, D_MODEL)), _full_spec((1, D_MODEL)), _full_spec((D_MODEL, D_MODEL)),
           _full_spec((D_MODEL, D_MODEL)), _full_spec((ple, D_MODEL))],
        scratch_shapes=scratch, name="attn_out_post", **_post_outputs(t, tm),
    )(*os_, *lses, z, x, p, ln_g, ln_b, w_out, w_pg, w_pe)


def _attn_prompt_kernel(q_ref, kp_ref, kc_ref, vp_ref, vc_ref, o_ref, lse_ref):
    ib = pl.program_id(2)
    tq = q_ref.shape[0]
    row = lax.broadcasted_iota(jnp.int32, (tq, 2 * tq), 0)
    col = lax.broadcasted_iota(jnp.int32, (tq, 2 * tq), 1)
    back = row + tq - col
    valid = (back >= 0) & (back <= WINDOW_KEYS) & ((col >= tq) | (ib > 0))
    lane = lax.broadcasted_iota(jnp.int32, (tq, LANES), 1)
    low = lane < HEAD_DIM
    head_of_lane = lane >> ((LANES // N_HEADS).bit_length() - 1)
    lse_tile = jnp.zeros((tq, LANES), F32)
    nt = (((1,), (1,)), ((), ()))
    for j in range(ATTN_WIDTH // LANES):
        sl = slice(j * LANES, (j + 1) * LANES)
        q2 = q_ref[:, sl]
        k2 = jnp.concatenate([kp_ref[:, sl], kc_ref[:, sl]], axis=0)
        v2 = jnp.concatenate([vp_ref[:, sl], vc_ref[:, sl]], axis=0)
        outs = []
        for sel in (low, jnp.logical_not(low)):
            qm = jnp.where(sel, q2, jnp.zeros_like(q2))
            s = lax.dot_general(qm, k2, nt, preferred_element_type=F32)
            s = jnp.where(valid, s, NEG_INF)
            m = jnp.max(s, axis=1, keepdims=True)
            pr = jnp.exp(s - m)
            l = jnp.sum(pr, axis=1, keepdims=True)
            o = jnp.dot(pr.astype(BF16), v2, preferred_element_type=F32)
            outs.append((o / l, m + jnp.log(l)))
        o_ref[:, sl] = jnp.where(low, outs[0][0], outs[1][0]).astype(o_ref.dtype)
        for hh in range(2):
            lse_tile = jnp.where(head_of_lane == 2 * j + hh, outs[hh][1], lse_tile)
    lse_ref[...] = lse_tile


def _attn_prompt_group(q, k, v, n_seq, seq_len):
    dil, rows, _ = q.shape
    tq = WINDOW_KEYS
    nb = seq_len // dil // tq
    assert rows == n_seq * nb * tq

    def cur(b, r, ib):
        return (r, b * nb + ib, 0)

    def prev(b, r, ib):
        return (r, b * nb + jnp.maximum(ib - 1, 0), 0)

    blk = (None, tq, D_MODEL)
    return pl.pallas_call(
        _attn_prompt_kernel, grid=(n_seq, dil, nb),
        in_specs=[pl.BlockSpec(blk, cur), pl.BlockSpec(blk, prev), pl.BlockSpec(blk, cur),
                  pl.BlockSpec(blk, prev), pl.BlockSpec(blk, cur)],
        out_specs=[pl.BlockSpec(blk, cur), pl.BlockSpec((None, tq, LANES), cur)],
        out_shape=[jax.ShapeDtypeStruct((dil, rows, D_MODEL), BF16), jax.ShapeDtypeStruct((dil, rows, LANES), F32)],
        compiler_params=pltpu.CompilerParams(dimension_semantics=("parallel", "parallel", "arbitrary"),
                                             vmem_limit_bytes=VMEM_LIMIT),
        name=f"attn_prompt_d{dil}",
    )(q, k, k, v, v)


SAMPLE_KEY_CHUNK = 512


def _attn_sample_kernel(q_ref, kvn_ref, c0_ref, c1_ref, c2_ref, o_ref, *rest, compact, emit):
    if emit:
        comp_refs, (s_scr, m_scr) = rest[:N_ATT_GROUPS], rest[N_ATT_GROUPS:]
    else:
        s_scr, m_scr = rest
    step = pl.program_id(1)
    ds = q_ref.shape[1]
    n_rows = N_HEADS * ds
    assert ds & (ds - 1) == 0 and n_rows == LANES
    rho = lax.broadcasted_iota(jnp.int32, (n_rows, ATTN_WIDTH), 0)
    lane = lax.broadcasted_iota(jnp.int32, (n_rows, ATTN_WIDTH), 1)
    head_mask = (rho >> (ds.bit_length() - 1)) == (lane >> (HEAD_DIM.bit_length() - 1))
    cache_refs = (c0_ref, c1_ref, c2_ref)
    sparse = [d > ds for d in DILATIONS]
    nt = (((1,), (1,)), ((), ()))
    kvn = kvn_ref[0].astype(F32)

    def load_chunk(g, c0, cw):
        blk = cache_refs[g][0, 0, :, c0:c0 + cw].astype(BF16)
        if emit:
            if sparse[g]:
                keep = cw * ds // DILATIONS[g]
                src = lax.broadcasted_iota(jnp.int32, (cw, keep), 0)
                dst = lax.broadcasted_iota(jnp.int32, (cw, keep), 1)
                picked = (dst >> (ds.bit_length() - 1)) * DILATIONS[g] + (dst & (ds - 1))
                sel = jnp.where(src == picked, 1.0, 0.0).astype(BF16)
                k0 = c0 * ds // DILATIONS[g]
                comp_refs[g][0, 0, :, k0:k0 + keep] = jnp.dot(blk, sel, preferred_element_type=F32).astype(BF16)
            else:
                comp_refs[g][0, 0, :, c0:c0 + cw] = blk
        return blk

    def cache_valid(g, c0, cw):
        qi = lax.broadcasted_iota(jnp.int32, (n_rows, cw), 0) & (ds - 1)
        col_idx = lax.broadcasted_iota(jnp.int32, (n_rows, cw), 1) + c0
        if compact and sparse[g]:
            return (col_idx & (ds - 1)) == qi
        back = col_idx - qi
        return (back >= 0) & ((back & (DILATIONS[g] - 1)) == 0)

    def new_block(g, off):
        piece = kvn[:, off + g * ATTN_WIDTH: off + (g + 1) * ATTN_WIDTH]
        pad = jnp.zeros((LANES - ds, ATTN_WIDTH), F32)
        return jnp.concatenate([piece, pad], axis=0).astype(BF16)

    chunks = []
    col = 0
    for g, c_ref in enumerate(cache_refs):
        width = c_ref.shape[3]
        for c0 in range(0, width, SAMPLE_KEY_CHUNK):
            cw = min(SAMPLE_KEY_CHUNK, width - c0)
            chunks.append((g, c0, cw, col))
            col += cw
    new_cols = [col + g * LANES for g in range(N_ATT_GROUPS)]

    @pl.when(step == 0)
    def _():
        q_all = q_ref[0].astype(F32)
        q_rows = []
        for g in range(N_ATT_GROUPS):
            qg = q_all[:, g * ATTN_WIDTH:(g + 1) * ATTN_WIDTH]
            q_rows.append(jnp.where(head_mask, jnp.concatenate([qg] * N_HEADS, axis=0), 0.0).astype(BF16))
        m = jnp.full((n_rows, 1), NEG_INF, F32)
        for g, c0, cw, col0 in chunks:
            s = jnp.dot(q_rows[g], load_chunk(g, c0, cw), preferred_element_type=F32)
            s = jnp.where(cache_valid(g, c0, cw), s, NEG_INF)
            s_scr[:, col0:col0 + cw] = s
            m = jnp.maximum(m, jnp.max(s, axis=1, keepdims=True))
        qi = lax.broadcasted_iota(jnp.int32, (n_rows, LANES), 0) & (ds - 1)
        kappa = lax.broadcasted_iota(jnp.int32, (n_rows, LANES), 1)
        for g in range(N_ATT_GROUPS):
            s = lax.dot_general(q_rows[g], new_block(g, 0), nt, preferred_element_type=F32)
            ahead = qi - kappa
            s = jnp.where((ahead >= 0) & ((ahead & (DILATIONS[g] - 1)) == 0) & (kappa < ds), s, NEG_INF)
            s_scr[:, new_cols[g]:new_cols[g] + LANES] = s
            m = jnp.maximum(m, jnp.max(s, axis=1, keepdims=True))
        m_scr[...] = jnp.broadcast_to(m, m_scr.shape)

    @pl.when(step == 1)
    def _():
        m = m_scr[:, 0:1]
        l = jnp.zeros((n_rows, 1), F32)
        acc = jnp.zeros((n_rows, ATTN_WIDTH), F32)
        for g, c0, cw, col0 in chunks:
            pr = jnp.exp(s_scr[:, col0:col0 + cw] - m)
            l = l + jnp.sum(pr, axis=1, keepdims=True)
            acc = acc + lax.dot_general(pr.astype(BF16), load_chunk(g, c0, cw), nt, preferred_element_type=F32)
        for g in range(N_ATT_GROUPS):
            pr = jnp.exp(s_scr[:, new_cols[g]:new_cols[g] + LANES] - m)
            l = l + jnp.sum(pr, axis=1, keepdims=True)
            acc = acc + jnp.dot(pr.astype(BF16), new_block(g, N_ATT_GROUPS * ATTN_WIDTH),
                                preferred_element_type=F32)
        acc = acc / l
        acc3 = acc.reshape(N_HEADS, ds, ATTN_WIDTH)
        low = lax.broadcasted_iota(jnp.int32, (ds, LANES), 1) < HEAD_DIM
        for c in range(ATTN_WIDTH // LANES):
            sl = slice(c * LANES, (c + 1) * LANES)
            o_ref[0, :, sl] = jnp.where(low, acc3[2 * c][:, sl], acc3[2 * c + 1][:, sl])


def _cache_views(caches):
    views = []
    for cache, dil in zip(caches, DILATIONS):
        n, width = cache.shape[:2]
        assert width == WINDOW_KEYS * dil and cache.shape[2:] == (2, N_HEADS, HEAD_DIM)
        views.append(cache.transpose(0, 2, 3, 4, 1).reshape(n, 2, ATTN_WIDTH, width))
    return views


def _attn_sample(qz_s, kv_s, views, n_seq, ds, *, compact, emit):
    q3 = qz_s.reshape(n_seq, ds, qz_s.shape[1])
    kv3 = kv_s.reshape(n_seq, ds, kv_s.shape[1])
    n_cols = sum(v.shape[3] for v in views) + N_ATT_GROUPS * LANES

    def tok_spec(width):
        return pl.BlockSpec((1, ds, width), lambda n, s: (n, 0, 0))

    def cache_spec(width):
        return pl.BlockSpec((1, 1, ATTN_WIDTH, width), lambda n, s: (n, s, 0, 0))

    out_specs = [tok_spec(ATTN_WIDTH)]
    out_shape = [jax.ShapeDtypeStruct((n_seq, ds, ATTN_WIDTH), F32)]
    if emit:
        assert not compact
        widths = [v.shape[3] * ds // d if d > ds else v.shape[3] for v, d in zip(views, DILATIONS)]
        out_specs += [cache_spec(w) for w in widths]
        out_shape += [jax.ShapeDtypeStruct((n_seq, 2, ATTN_WIDTH, w), BF16) for w in widths]
    outs = pl.pallas_call(
        functools.partial(_attn_sample_kernel, compact=compact, emit=emit), grid=(n_seq, 2),
        in_specs=[tok_spec(q3.shape[2]), tok_spec(kv3.shape[2])] + [cache_spec(v.shape[3]) for v in views],
        out_specs=out_specs, out_shape=out_shape,
        scratch_shapes=[pltpu.VMEM((N_HEADS * ds, n_cols), F32), pltpu.VMEM((N_HEADS * ds, LANES), F32)],
        compiler_params=pltpu.CompilerParams(dimension_semantics=("parallel", "arbitrary"),
                                             vmem_limit_bytes=VMEM_LIMIT),
        name="attn_sample_compact" if compact else "attn_sample",
    )(q3, kv3, *views)
    return outs[0].reshape(n_seq * ds, ATTN_WIDTH), list(outs[1:])


def kernel(x_prompt, x_sample, state_ssm_re, state_ssm_im, cache_kv_w128, cache_kv_w512, cache_kv_w2048, p_prompt, p_sample, ln_g, ln_b, w_pe, w_pg, w_in_a, a_re, a_im, log_dt, b_re, b_im, c_re, c_im, d_skip, w_glu, w_out_a, w_kv, w_in_b, w_out_b):
    depth = ln_g.shape[0]
    n_a = w_in_a.shape[0]
    batch, seq, _ = x_prompt.shape
    n_s, ds, _ = x_sample.shape
    past = cache_kv_w2048.shape[1]
    assert ds * 2 == SSM_CHUNK and ds <= DILATIONS[2] and past == WINDOWS[2] and seq % (DILATIONS[2] * WINDOW_KEYS) == 0
    alpha = (2 * depth) ** 0.25
    t_p, t_s = batch * seq, n_s * ds
    caches =(cache_kv_w128, cache_kv_w512, cache_kv_w2048)

    xs = [x_prompt.reshape(t_p, D_MODEL), x_sample.reshape(t_s, D_MODEL)]
    xbs = list(xs)
    ps = [p_prompt.reshape(depth, t_p, -1), p_sample.reshape(depth, t_s, -1)]
    wb = lambda w: w.astype(BF16)
    h_prompt, h_sample = [], []

    for i in range(n_a):
        w_in = wb(w_in_a[i])
        (xg, z_p), (xsg, z_s) = _project_ssm(xbs[0], w_in, SSM_CHUNK), _project_ssm(xbs[1], w_in, ds)
        ops = _ssm_operators(a_re[i], a_im[i], log_dt[i], b_re[i], b_im[i], c_re[i], c_im[i], d_skip[i])
        h0g = jnp.concatenate([state_ssm_re[i], state_ssm_im[i]], axis=-1).astype(F32).transpose(1, 0, 2)
        yg, hl, ysg, hs = _ssm(xg, xsg, h0g, ops, n_seq=batch)
        h_prompt.append(hl.transpose(1, 0, 2))
        h_sample.append(hs.transpose(1, 0, 2))
        for k, (y, z) in enumerate(((yg, z_p), (ysg, z_s))):
            xs[k], xbs[k] = _glu_post(y, z, xs[k], ps[k], i, ln_g[i:i + 1], ln_b[i:i + 1], wb(w_glu[i]),
                                      wb(w_out_a[i]), wb(w_pg[i]), wb(w_pe[i]), alpha)

    pos = [jnp.arange(seq, dtype=F32), jnp.tile(past + jnp.arange(ds, dtype=F32), n_s)]
    rope_k = [_rope_tables(p_, 1.0) for p_ in pos]
    rope_q = [tuple(HEAD_DIM ** -0.5 * tab for tab in tabs) for tabs in rope_k]
    q_width = N_ATT_GROUPS * ATTN_WIDTH
    n_grp = N_ATT_GROUPS
    w_kv_b = wb(w_kv)
    k_p = [_project_deint(xbs[0], w_kv_b, grp, DILATIONS[grp], rope_k[0]) for grp in range(n_grp)]
    v_p = [_project_deint(xbs[0], w_kv_b, n_grp + grp, DILATIONS[grp]) for grp in range(n_grp)]
    kv_s = _project(xbs[1], w_kv_b, rope=rope_k[1], rot_cols=q_width, tn=512)
    views = _cache_views(caches)

    for i in range(n_a, depth):
        j = i - n_a
        w_in = wb(w_in_b[j])
        post_w = (ln_g[i:i + 1], ln_b[i:i + 1], wb(w_out_b[j]), wb(w_pg[i]), wb(w_pe[i]), alpha)
        q_p = [_project_deint(xbs[0], w_in, grp, DILATIONS[grp], rope_q[0]) for grp in range(n_grp)]
        z_p = _project_deint(xbs[0], w_in, n_grp, 1).reshape(t_p, D_MODEL)
        parts = [_attn_prompt_group(q_p[grp], k_p[grp], v_p[grp], batch, seq) for grp in range(n_grp)]
        qz_s = _project(xbs[1], w_in, rope=rope_q[1], rot_cols=q_width, tn=512)
        first, last = i == n_a, i == depth - 1
        o_s, compacted = _attn_sample(qz_s, kv_s, views, n_s, ds, compact=not first, emit=first and not last)
        if compacted:
            views = compacted
        xs[0], xbs[0] = _comb_post([pt[0] for pt in parts], [pt[1] for pt in parts], DILATIONS, z_p, 0,
                                   xs[0], ps[0], i, *post_w)
        xs[1], xbs[1] = _comb_post([o_s], [], (1,), qz_s, n_grp, xs[1], ps[1], i, *post_w)

    y_prompt = xs[0].reshape(batch, seq, D_MODEL)
    y_sample = xs[1].reshape(n_s, ds, D_MODEL)
    hp = jnp.stack(h_prompt)
    hsm = jnp.stack(h_sample)

    def prompt_window(arr, dil, width):
        per = arr.reshape(dil, batch, seq // dil, ATTN_WIDTH)[:, :, (seq - width) // dil:]
        return per.transpose(1, 2, 0, 3).reshape(batch, width, N_HEADS, HEAD_DIM).astype(F32)

    kv_prompt = [jnp.stack([prompt_window(k_p[grp], DILATIONS[grp], min(w, seq)),
                            prompt_window(v_p[grp], DILATIONS[grp], min(w, seq))], axis=2)
                 for grp, w in enumerate(WINDOWS)]
    kv_s6 = kv_s.astype(F32).reshape(n_s, ds, 2, N_ATT_GROUPS, N_HEADS, HEAD_DIM)
    kv_sample = [kv_s6[:, :, :, grp] for grp in range(N_ATT_GROUPS)]
    return (y_prompt, y_sample, hp[..., :SSM_STATE], hp[..., SSM_STATE:], hsm[..., :SSM_STATE], hsm[..., SSM_STATE:],
            *kv_prompt, *kv_sample)
```

```python
import functools
import math

import jax
import jax.numpy as jnp
from jax import lax
from jax.experimental import pallas as pl
from jax.experimental.pallas import tpu as pltpu

F32 = jnp.float32
BF16 = jnp.bfloat16

D_MODEL = 1024
N_SSM_GROUPS = 64
SSM_GROUP = 16
SSM_STATE = 64
SSM_CHUNK = 16
N_ATT_GROUPS = 3
WINDOWS = (128, 512, 2048)
DILATIONS = (1, 4, 16)
HEAD_DIM = 64
N_HEADS = 16
ATTN_WIDTH = N_HEADS * HEAD_DIM
ROT_DIM = HEAD_DIM // 4
ROPE_THETA = 500000.0
WINDOW_KEYS = 128
LN_EPS = 1e-5
NEG_INF = -1e30
LANES = 128
SUBLANES = 8
VMEM_LIMIT = 52 * 1024 * 1024


def _sigmoid(x):
    return 1.0 / (1.0 + jnp.exp(-x))


def _gelu_tanh(x):
    return 0.5 * x * (1.0 + jnp.tanh(math.sqrt(2.0 / math.pi) * (x + 0.044715 * (x * x * x))))


def _mm_kernel(x_ref, w_ref, o_ref):
    o_ref[...] = jnp.dot(x_ref[...], w_ref[...], preferred_element_type=F32).astype(o_ref.dtype)


def _mm_rope_kernel(x_ref, w_ref, cos_ref, sa_ref, sb_ref, o_ref, *, rot_blocks):
    j = pl.program_id(1)
    acc = jnp.dot(x_ref[...], w_ref[...], preferred_element_type=F32)

    @pl.when(j < rot_blocks)
    def _():
        o_ref[...] = _rope(acc, cos_ref[...], sa_ref[...], sb_ref[...]).astype(o_ref.dtype)

    @pl.when(j >= rot_blocks)
    def _():
        o_ref[...] = acc.astype(o_ref.dtype)


def _token_tile(n_tokens, want):
    tm = min(want, n_tokens)
    assert n_tokens % tm == 0
    return tm


def _project(x, w, *, rope=None, rot_cols=0, tn=512, out_dtype=BF16):
    t, k = x.shape
    n = w.shape[1]
    tm = _token_tile(t, 1024)
    grid = (t // tm, n // tn)
    x_spec = pl.BlockSpec((tm, k), lambda i, j: (i, 0))
    w_spec = pl.BlockSpec((k, tn), lambda i, j: (0, j))
    o_spec = pl.BlockSpec((tm, tn), lambda i, j: (i, j))
    params = pltpu.CompilerParams(dimension_semantics=("parallel", "arbitrary"), vmem_limit_bytes=VMEM_LIMIT)
    if rope is None:
        return pl.pallas_call(
            _mm_kernel, grid=grid, in_specs=[x_spec, w_spec], out_specs=o_spec,
            out_shape=jax.ShapeDtypeStruct((t, n), out_dtype), compiler_params=params, name="proj",
        )(x, w)
    assert rot_cols % tn == 0
    tab_spec = pl.BlockSpec((tm, LANES), lambda i, j: (i, 0))
    return pl.pallas_call(
        functools.partial(_mm_rope_kernel, rot_blocks=rot_cols // tn), grid=grid,
        in_specs=[x_spec, w_spec, tab_spec, tab_spec, tab_spec], out_specs=o_spec,
        out_shape=jax.ShapeDtypeStruct((t, n), out_dtype), compiler_params=params, name="proj_rope",
    )(x, w, *rope)


PROJ_SUB_ROWS = 512
DEINT_STAGE = 4


def _rope(acc, cos, sa, sb):
    half = ROT_DIM // 2
    outs = []
    for c in range(acc.shape[1] // LANES):
        a = acc[:, c * LANES:(c + 1) * LANES]
        outs.append(a * cos + pltpu.roll(a, LANES - half, 1) * sa + pltpu.roll(a, half, 1) * sb)
    return jnp.concatenate(outs, axis=1)


def _mm_deint_kernel(*refs, dil, rope):
    if rope:
        x_ref, w_ref, cos_ref, sa_ref, sb_ref, o_ref = refs[:6]
    else:
        x_ref, w_ref, o_ref = refs[:3]
    tm = x_ref.shape[0]
    sub = min(tm, PROJ_SUB_ROWS)
    for b in range(tm // sub):
        rs = slice(b * sub, (b + 1) * sub)
        acc = jnp.dot(x_ref[rs, :], w_ref[...], preferred_element_type=F32)
        if rope:
            acc = _rope(acc, cos_ref[rs, :], sa_ref[rs, :], sb_ref[rs, :])
        if dil == 1:
            o_ref[0, rs, :] = acc.astype(o_ref.dtype)
            continue
        acc_scr, mid_scr = refs[-2:]
        rows = sub // dil
        f1 = min(dil, DEINT_STAGE)
        f2 = dil // f1
        for c in range(acc.shape[1] // LANES):
            sl = slice(c * LANES, (c + 1) * LANES)
            acc_scr[c, rs, :] = acc[:, sl]
            for r1 in range(f1):
                if f2 == 1:
                    o_ref[r1, b * rows:(b + 1) * rows, sl] = (
                        acc_scr[c, pl.ds(b * sub + r1, rows, stride=dil), :].astype(o_ref.dtype))
                    continue
                mid_scr[r1] = acc_scr[c, pl.ds(b * sub + r1, sub // f1, stride=f1), :]
                for r2 in range(f2):
                    o_ref[r1 + f1 * r2, b * rows:(b + 1) * rows, sl] = (
                        mid_scr[r1, pl.ds(r2, rows, stride=f2), :].astype(o_ref.dtype))


def _project_deint(x, w, col, dil, rope=None):
    t, k = x.shape
    tm = _token_tile(t, 1024)
    wn = D_MODEL
    in_specs = [pl.BlockSpec((tm, k), lambda i: (i, 0)), pl.BlockSpec((k, wn), lambda i: (0, col))]
    args = [x, w]
    if rope is not None:
        tab_blocks = rope[0].shape[0] // tm
        in_specs += [pl.BlockSpec((tm, LANES), lambda i: (i % tab_blocks, 0))] * 3
        args += list(rope)
    return pl.pallas_call(
        functools.partial(_mm_deint_kernel, dil=dil, rope=rope is not None), grid=(t // tm,),
        in_specs=in_specs, out_specs=pl.BlockSpec((dil, tm // dil, wn), lambda i: (0, i, 0)),
        out_shape=jax.ShapeDtypeStruct((dil, t // dil, wn), BF16),
        scratch_shapes=[pltpu.VMEM((wn // LANES, tm, LANES), F32),
                        pltpu.VMEM((DEINT_STAGE, min(tm, PROJ_SUB_ROWS) // DEINT_STAGE, LANES), F32)] if dil > 1 else [],
        compiler_params=pltpu.CompilerParams(dimension_semantics=("parallel",), vmem_limit_bytes=VMEM_LIMIT),
        name=f"proj_deint{dil}",
    )(*args)


GROUPS_PER_SLAB = LANES // SSM_GROUP


def _slot_position(lanes_shape):
    return lax.broadcasted_iota(jnp.int32, lanes_shape, 1) >> (SSM_GROUP.bit_length() - 1)


def _mm_ssm_kernel(x_ref, w_ref, xg_ref, z_ref, slab_scr, mid_scr, *, chunk):
    tm = x_ref.shape[0]
    n_chunks = tm // chunk
    x = x_ref[...].astype(BF16)
    z_ref[...] = jnp.dot(x, w_ref[:, D_MODEL:], preferred_element_type=F32).astype(z_ref.dtype)
    u = jnp.dot(x, w_ref[:, :D_MODEL], preferred_element_type=F32)
    slot = _slot_position((n_chunks, LANES))
    gps = GROUPS_PER_SLAB
    f1 = DEINT_STAGE
    f2 = chunk // f1
    for c in range(D_MODEL // LANES):
        slab_scr[c] = u[:, c * LANES:(c + 1) * LANES]
        for q in range(f1):
            mid_scr[q] = slab_scr[c, pl.ds(q, tm // f1, stride=f1), :]
        for j in range(chunk // gps):
            rot = []
            for r in range(gps):
                s = gps * j + r
                rows = mid_scr[s % f1, pl.ds(s // f1, n_chunks, stride=f2), :]
                rot.append(rows if r == 0 else pltpu.roll(rows, r * SSM_GROUP, 1))
            for gg in range(gps):
                tile = rot[0]
                for r in range(1, gps):
                    tile = jnp.where(slot == (gg + r) % gps, rot[r], tile)
                xg_ref[c * gps + gg, :, j * LANES:(j + 1) * LANES] = tile.astype(xg_ref.dtype)


def _project_ssm(x, w, chunk):
    t, k = x.shape
    tm = _token_tile(t, 512)
    cw = chunk * SSM_GROUP
    return pl.pallas_call(
        functools.partial(_mm_ssm_kernel, chunk=chunk), grid=(t // tm,),
        in_specs=[pl.BlockSpec((tm, k), lambda i: (i, 0)), pl.BlockSpec(w.shape, lambda i: (0, 0))],
        out_specs=[pl.BlockSpec((N_SSM_GROUPS, tm // chunk, cw), lambda i: (0, i, 0)),
                   pl.BlockSpec((tm, D_MODEL), lambda i: (i, 0))],
        out_shape=[jax.ShapeDtypeStruct((N_SSM_GROUPS, t // chunk, cw), BF16),
                   jax.ShapeDtypeStruct((t, D_MODEL), BF16)],
        scratch_shapes=[pltpu.VMEM((D_MODEL // LANES, tm, LANES), F32),
                        pltpu.VMEM((DEINT_STAGE, tm // DEINT_STAGE, LANES), F32)],
        compiler_params=pltpu.CompilerParams(dimension_semantics=("parallel",), vmem_limit_bytes=VMEM_LIMIT),
        name=f"proj_ssm{chunk}",
    )(x, w)


def _rope_tables(pos, scale):
    half = ROT_DIM // 2
    inv_freq = ROPE_THETA ** (-jnp.arange(0, ROT_DIM, 2, dtype=F32) / ROT_DIM)
    ang = pos[:, None] * inv_freq[None, :]
    cos, sin = jnp.cos(ang), jnp.sin(ang)
    t = pos.shape[0]
    ones = jnp.ones((t, HEAD_DIM - ROT_DIM), F32)
    zeros = jnp.zeros((t, HEAD_DIM - ROT_DIM), F32)
    zhalf = jnp.zeros((t, half), F32)
    cos64 = jnp.concatenate([cos, cos, ones], axis=1)
    sa64 = jnp.concatenate([-sin, zhalf, zeros], axis=1)
    sb64 = jnp.concatenate([zhalf, sin, zeros], axis=1)
    return tuple(scale * jnp.concatenate([a, a], axis=1) for a in (cos64, sa64, sb64))


COEF_BLOCK_STEPS = tuple(SSM_CHUNK << k for k in range(3))
COEF_CARRY_STEPS = tuple(SSM_CHUNK * SUBLANES << k for k in range(6))
COEF_HALF_ROW = len(COEF_BLOCK_STEPS) + len(COEF_CARRY_STEPS)
COEF_IN_BLOCK_ROW = 16
N_COEF_ROWS = COEF_IN_BLOCK_ROW + SUBLANES


def _cpair_mul(a, b):
    return a[0] * b[0] - a[1] * b[1], a[0] * b[1] + a[1] * b[0]


def _abar_and_factor(a_re, a_im, dt):
    mag = jnp.exp(dt * a_re)
    ab = (mag * jnp.cos(dt * a_im), mag * jnp.sin(dt * a_im))
    den = a_re * a_re + a_im * a_im
    nr, ni = ab[0] - 1.0, ab[1]
    fac = ((nr * a_re + ni * a_im) / den, (ni * a_re - nr * a_im) / den)
    return ab, fac


def _powers(ab, n):
    out = [(jnp.ones_like(ab[0]), jnp.zeros_like(ab[0]))]
    for _ in range(n):
        out.append(_cpair_mul(out[-1], ab))
    return out


OPS_GROUPS_PER_STEP = 1


def _ssm_ops_kernel(*refs):
    io_refs, (rhs_scr, t_scr) = refs[:-2], refs[-2:]
    for j in range(OPS_GROUPS_PER_STEP):
        group = pl.program_id(0) * OPS_GROUPS_PER_STEP + j
        _ssm_ops_group(group, *[r.at[pl.ds(j, 1)] for r in io_refs], rhs_scr.at[j], t_scr.at[j])


def _roll_lane_tiles(x, shift):
    tiles = [pltpu.roll(x[:, k * LANES:(k + 1) * LANES], shift, 1) for k in range(x.shape[1] // LANES)]
    return tiles[0] if len(tiles) == 1 else jnp.concatenate(tiles, axis=1)


def _ssm_ops_group(group, arow_ref, acol_ref, bt_ref, ct_ref, d_ref, t_ref, bin_ref, cout_ref, ca_ref, cb_ref,
                   rhs_scr, t_scr):
    c, m, p = SSM_CHUNK, SSM_GROUP, SSM_STATE
    gps = GROUPS_PER_SLAB
    hi = lax.Precision.HIGHEST
    rot = group & (gps - 1)

    def slot_row(s):
        return pl.multiple_of((s // gps * gps + ((rot + s % gps) & (gps - 1))) * m, m)

    ab, fac = _abar_and_factor(arow_ref[0, 0:1, :], arow_ref[0, 1:2, :], arow_ref[0, 2:3, :])
    bbar = _cpair_mul((bt_ref[0, 0], bt_ref[0, 1]), fac)
    pw = _powers(ab, c)
    for s in range(c):
        blk = _cpair_mul(bbar, pw[c - 1 - s])
        bin_ref[0, pl.ds(slot_row(s), m), 0:p] = blk[0].astype(bin_ref.dtype)
        bin_ref[0, pl.ds(slot_row(s), m), p:2 * p] = blk[1].astype(bin_ref.dtype)
    coef = {}
    cur = pw[c]
    for step in COEF_BLOCK_STEPS + COEF_CARRY_STEPS:
        coef[step] = cur
        cur = _cpair_mul(cur, cur)
    rows = [coef[s] for s in COEF_BLOCK_STEPS + COEF_CARRY_STEPS] + [pw[c // 2]]
    rows += [pw[0]] * (COEF_IN_BLOCK_ROW - len(rows))
    in_block = [pw[0]]
    for _ in range(SUBLANES - 1):
        in_block.append(_cpair_mul(in_block[-1], pw[c]))
    rows += in_block
    for k, (re, im) in enumerate(rows):
        ca_ref[0, k:k + 1, 0:p] = re
        ca_ref[0, k:k + 1, p:2 * p] = re
        cb_ref[0, k:k + 1, 0:p] = -im
        cb_ref[0, k:k + 1, p:2 * p] = im

    abc, _ = _abar_and_factor(acol_ref[0, :, 0:1], acol_ref[0, :, 1:2], acol_ref[0, :, 2:3])
    pwc = _powers(abc, c)
    ct = (ct_ref[0, 0], ct_ref[0, 1])
    for t in range(c + 1):
        blk = _cpair_mul(ct, pwc[t])
        if t < c:
            rhs_scr[0, :, t * m:(t + 1) * m] = blk[0]
            rhs_scr[1, :, t * m:(t + 1) * m] = blk[1]
        if t >= 1:
            rhs_scr[2, :, (t - 1) * m:t * m] = blk[0]
            rhs_scr[3, :, (t - 1) * m:t * m] = -blk[1]
    lane_shift = rot * m
    cout_ref[0, 0:p, :] = _roll_lane_tiles(rhs_scr[2], lane_shift).astype(cout_ref.dtype)
    cout_ref[0, p:2 * p, :] = _roll_lane_tiles(rhs_scr[3], lane_shift).astype(cout_ref.dtype)

    kern = (jnp.dot(bbar[0], rhs_scr[0], precision=hi, preferred_element_type=F32)
            - jnp.dot(bbar[1], rhs_scr[1], precision=hi, preferred_element_type=F32))
    lane = lax.broadcasted_iota(jnp.int32, kern.shape, 1)
    row = lax.broadcasted_iota(jnp.int32, kern.shape, 0)
    for s in range(c):
        shifted = kern if s == 0 else jnp.where(lane >= s * m, pltpu.roll(kern, s * m, 1), 0.0)
        shifted = shifted + jnp.where(lane == row + s * m, d_ref[0], 0.0)
        t_scr[pl.ds(slot_row(s), m), :] = shifted
    t_ref[0] = _roll_lane_tiles(t_scr[...], lane_shift).astype(t_ref.dtype)


def _ssm_operators(a_re, a_im, log_dt, b_re, b_im, c_re, c_im, d_skip):
    g, p, m, c = N_SSM_GROUPS, SSM_STATE, SSM_GROUP, SSM_CHUNK
    dt = jnp.broadcast_to(jnp.exp(log_dt.astype(F32))[:, None], (g, p))
    arow = jnp.stack([a_re.astype(F32), a_im.astype(F32), dt], axis=1)
    acol = arow.transpose(0, 2, 1)
    bt = jnp.stack([b_re, b_im], axis=1).astype(F32).transpose(0, 1, 3, 2)
    ct = jnp.stack([c_re, c_im], axis=1).astype(F32).transpose(0, 1, 3, 2)
    d_rep = jnp.tile(d_skip.astype(F32).reshape(g, 1, m), (1, 1, c))
    cw = c * m

    gps = OPS_GROUPS_PER_STEP

    def spec(*shape):
        return pl.BlockSpec((gps,) + shape, lambda i: (i,) + (0,) * len(shape))

    return pl.pallas_call(
        _ssm_ops_kernel, grid=(g // gps,),
        in_specs=[spec(3, p), spec(p, 3), spec(2, m, p), spec(2, p, m), spec(1, cw)],
        out_specs=[spec(cw, cw), spec(cw, 2 * p), spec(2 * p, cw), spec(N_COEF_ROWS, 2 * p), spec(N_COEF_ROWS, 2 * p)],
        out_shape=[jax.ShapeDtypeStruct((g, cw, cw), BF16), jax.ShapeDtypeStruct((g, cw, 2 * p), BF16),
                   jax.ShapeDtypeStruct((g, 2 * p, cw), BF16), jax.ShapeDtypeStruct((g, N_COEF_ROWS, 2 * p), F32),
                   jax.ShapeDtypeStruct((g, N_COEF_ROWS, 2 * p), F32)],
        scratch_shapes=[pltpu.VMEM((gps, 4, p, cw), F32), pltpu.VMEM((gps, cw, cw), F32)],
        compiler_params=pltpu.CompilerParams(dimension_semantics=("parallel",)),
        name="s5_operators",
    )(arow, acol, bt, ct, d_rep)


def _swap_halves(x):
    return pltpu.roll(x, SSM_STATE, x.ndim - 1)


def _cmul(x, xs, ca, cb):
    return x * ca + xs * cb


def _ssm_kernel(x_ref, xs_ref, h0_ref, t_ref, bin_ref, cout_ref, ca_ref, cb_ref,
                y_ref, hl_ref, ys_ref, hs_ref, l_scr, e_scr, *, n_seq, rows_per_seq):
    rows = n_seq * rows_per_seq
    n_blocks = rows_per_seq // SUBLANES
    ca = ca_ref[0]
    cb = cb_ref[0]
    x = x_ref[0]
    v = jnp.dot(x, bin_ref[0], preferred_element_type=F32)
    rmod = lax.broadcasted_iota(jnp.int32, (rows, LANES), 0) & (SUBLANES - 1)

    for k, dist in enumerate((1, 2, 4)):
        sh = jnp.where(rmod >= dist, pltpu.roll(v, dist, 0), 0.0)
        v = v + _cmul(sh, _swap_halves(sh), ca[k:k + 1], cb[k:k + 1])
    l_scr[...] = v

    assert n_blocks & (n_blocks - 1) == 0 and n_blocks <= 1 << len(COEF_CARRY_STEPS)
    e = l_scr[pl.ds(SUBLANES - 1, rows // SUBLANES, stride=SUBLANES), :]
    bmod = lax.broadcasted_iota(jnp.int32, e.shape, 0) & (n_blocks - 1)
    for k in range(n_blocks.bit_length() - 1):
        row = len(COEF_BLOCK_STEPS) + k
        sh = jnp.where(bmod >= (1 << k), pltpu.roll(e, 1 << k, 0), 0.0)
        e = e + _cmul(sh, _swap_halves(sh), ca[row:row + 1], cb[row:row + 1])
    hl_ref[0] = jnp.concatenate([e[(s + 1) * n_blocks - 1:(s + 1) * n_blocks] for s in range(n_seq)], axis=0)
    e_prev = jnp.where(bmod >= 1, pltpu.roll(e, 1, 0), 0.0)
    for r in range(SUBLANES):
        e_scr[pl.ds(r, rows // SUBLANES, stride=SUBLANES), :] = e_prev

    lshift = jnp.where(rmod >= 1, pltpu.roll(v, 1, 0), 0.0)
    ebc = e_scr[...]
    pwa = ca[COEF_IN_BLOCK_ROW:COEF_IN_BLOCK_ROW + SUBLANES]
    pwb = cb[COEF_IN_BLOCK_ROW:COEF_IN_BLOCK_ROW + SUBLANES]
    shape3 = (rows // SUBLANES, SUBLANES, LANES)
    h = (lshift.reshape(shape3) + ebc.reshape(shape3) * pwa[None]
         + _swap_halves(ebc).reshape(shape3) * pwb[None]).reshape(rows, LANES)
    y = jnp.dot(x, t_ref[0], preferred_element_type=F32)
    y = y + jnp.dot(h.astype(BF16), cout_ref[0], preferred_element_type=F32)
    y_ref[0] = y.astype(y_ref.dtype)

    half = SSM_CHUNK * SSM_GROUP // 2
    xs = xs_ref[0]
    h0 = h0_ref[0]
    ys = jnp.dot(xs, t_ref[0, :half, :half], preferred_element_type=F32)
    ys = ys + jnp.dot(h0.astype(BF16), cout_ref[0, :, :half], preferred_element_type=F32)
    ys_ref[0] = ys.astype(ys_ref.dtype)
    hs = _cmul(h0, _swap_halves(h0), ca[COEF_HALF_ROW:COEF_HALF_ROW + 1], cb[COEF_HALF_ROW:COEF_HALF_ROW + 1])
    hs_ref[0] = hs + jnp.dot(xs, bin_ref[0, half:, :], preferred_element_type=F32)


def _ssm(xg, xsg, h0g, ops, *, n_seq):
    t_op, b_in, c_out, ca, cb = ops
    g, rows, cw = xg.shape
    n_s = xsg.shape[1]
    hw = xsg.shape[2]

    def spec(shape):
        return pl.BlockSpec((1,) + shape, lambda i: (i, 0, 0))

    return pl.pallas_call(
        functools.partial(_ssm_kernel, n_seq=n_seq, rows_per_seq=rows // n_seq),
        grid=(g,),
        in_specs=[spec((rows, cw)), spec((n_s, hw)), spec((n_s, LANES)), spec((cw, cw)), spec((cw, LANES)),
                  spec((LANES, cw)), spec((N_COEF_ROWS, LANES)), spec((N_COEF_ROWS, LANES))],
        out_specs=[spec((rows, cw)), spec((n_seq, LANES)), spec((n_s, hw)), spec((n_s, LANES))],
        out_shape=[jax.ShapeDtypeStruct((g, rows, cw), BF16), jax.ShapeDtypeStruct((g, n_seq, LANES), F32),
                   jax.ShapeDtypeStruct((g, n_s, hw), BF16), jax.ShapeDtypeStruct((g, n_s, LANES), F32)],
        scratch_shapes=[pltpu.VMEM((rows, LANES), F32)] * 2,
        compiler_params=pltpu.CompilerParams(dimension_semantics=("parallel",), vmem_limit_bytes=VMEM_LIMIT),
        name="s5_scan",
    )(xg, xsg, h0g, t_op, b_in, c_out, ca, cb)


def _post_layer(x, sub, p, g_ref, b_ref, wpg_ref, wpe_ref, alpha):
    r = alpha * x + sub
    mu = jnp.mean(r, axis=-1, keepdims=True)
    cen = r - mu
    var = jnp.mean(cen * cen, axis=-1, keepdims=True)
    h = cen * lax.rsqrt(var + LN_EPS) * g_ref[...] + b_ref[...]
    gate = _sigmoid(jnp.dot(h.astype(BF16), wpg_ref[...], preferred_element_type=F32))
    ple = jnp.dot(p.astype(BF16), wpe_ref[...], preferred_element_type=F32)
    return h + gate * ple


def _glu_post_kernel(yg_ref, z_ref, x_ref, p_ref, g_ref, b_ref, wglu_ref, wout_ref, wpg_ref, wpe_ref,
                     xo_ref, xb_ref, slab_scr, mid_scr, *, alpha, chunk):
    n_chunks = yg_ref.shape[1]
    slot = _slot_position((n_chunks, LANES))
    gps = GROUPS_PER_SLAB
    f1 = DEINT_STAGE
    f2 = chunk // f1
    tm = n_chunks * chunk
    for c in range(D_MODEL // LANES):
        for j in range(chunk // gps):
            tiles = [yg_ref[c * gps + gg, :, j * LANES:(j + 1) * LANES].astype(F32) for gg in range(gps)]
            for r in range(gps):
                picked = tiles[0]
                for gg in range(1, gps):
                    picked = jnp.where(slot == (gg + r) % gps, tiles[gg], picked)
                rows = picked if r == 0 else pltpu.roll(picked, LANES - r * SSM_GROUP, 1)
                t = gps * j + r
                mid_scr[t % f1, pl.ds(t // f1, n_chunks, stride=f2), :] = rows
        for q in range(f1):
            slab_scr[c, pl.ds(q, tm // f1, stride=f1), :] = mid_scr[q]
    y = jnp.concatenate([slab_scr[c] for c in range(D_MODEL // LANES)], axis=1)
    g = _gelu_tanh(y)
    t = jnp.dot(g.astype(BF16), wglu_ref[...], preferred_element_type=F32)
    z = z_ref[...].astype(F32)
    glu = g * _sigmoid(t) * (z * _sigmoid(z))
    sub = jnp.dot(glu.astype(BF16), wout_ref[...], preferred_element_type=F32)
    out = _post_layer(x_ref[...], sub, p_ref[...], g_ref, b_ref, wpg_ref, wpe_ref, alpha)
    xo_ref[...] = out
    xb_ref[...] = out.astype(BF16)


def _interleave_rows(ref, scr, dil):
    if dil == 1:
        return ref[0].astype(F32)
    rows = ref.shape[1]
    slabs = []
    for c in range(ref.shape[2] // LANES):
        for r in range(dil):
            scr[c, pl.ds(r, rows, stride=dil), :] = ref[r, :, c * LANES:(c + 1) * LANES].astype(F32)
        slabs.append(scr[c])
    return slabs[0] if len(slabs) == 1 else jnp.concatenate(slabs, axis=1)


def _comb_post_kernel(*refs, dils, alpha):
    n = len(dils)
    if n == 1:
        o_ref = refs[0]
        z_ref, x_ref, p_ref, g_ref, b_ref, wout_ref, wpg_ref, wpe_ref, xo_ref, xb_ref = refs[1:]
        o = o_ref[...].astype(F32)
    else:
        o_refs, lse_refs = refs[:n], refs[n:2 * n]
        z_ref, x_ref, p_ref, g_ref, b_ref, wout_ref, wpg_ref, wpe_ref, xo_ref, xb_ref = refs[2 * n:2 * n + 10]
        scrs = list(refs[2 * n + 10:])
        lses = [_interleave_rows(r, scrs.pop(0) if d > 1 else None, d) for r, d in zip(lse_refs, dils)]
        top = functools.reduce(jnp.maximum, lses)
        ws = [jnp.exp(l - top) for l in lses]
        tot = functools.reduce(lambda a, c: a + c, ws)
        src = lax.broadcasted_iota(jnp.int32, (LANES, ATTN_WIDTH), 0)
        dst = lax.broadcasted_iota(jnp.int32, (LANES, ATTN_WIDTH), 1)
        lanes_per_head = LANES // N_HEADS
        expand = jnp.where(src == (dst >> (HEAD_DIM.bit_length() - 1)) * lanes_per_head, 1.0, 0.0).astype(BF16)
        o = None
        for w, r, d in zip(ws, o_refs, dils):
            a = w / tot
            hi = a.astype(BF16)
            lo = (a - hi.astype(F32)).astype(BF16)
            wide = (jnp.dot(hi, expand, preferred_element_type=F32) + jnp.dot(lo, expand, preferred_element_type=F32))
            term = wide * _interleave_rows(r, scrs.pop(0) if d > 1 else None, d)
            o = term if o is None else o + term
    z = z_ref[...].astype(F32)
    gated = o * (z * _sigmoid(z))
    sub = jnp.dot(gated.astype(BF16), wout_ref[...], preferred_element_type=F32)
    out = _post_layer(x_ref[...], sub, p_ref[...], g_ref, b_ref, wpg_ref, wpe_ref, alpha)
    xo_ref[...] = out
    xb_ref[...] = out.astype(BF16)


def _tok_spec(tm, width, col=0):
    return pl.BlockSpec((tm, width), lambda i: (i, col))


def _full_spec(shape):
    return pl.BlockSpec(shape, lambda i: (0,) * len(shape))


def _post_outputs(t, tm):
    return dict(
        out_specs=[_tok_spec(tm, D_MODEL), _tok_spec(tm, D_MODEL)],
        out_shape=[jax.ShapeDtypeStruct((t, D_MODEL), F32), jax.ShapeDtypeStruct((t, D_MODEL), BF16)],
        compiler_params=pltpu.CompilerParams(dimension_semantics=("parallel",), vmem_limit_bytes=VMEM_LIMIT),
    )


def _glu_post(yg, z, x, p, layer, ln_g, ln_b, w_glu, w_out, w_pg, w_pe, alpha):
    t = x.shape[0]
    tm = _token_tile(t, 512)
    ple = p.shape[2]
    n_grp, _, cw = yg.shape
    chunk = cw // SSM_GROUP
    return pl.pallas_call(
        functools.partial(_glu_post_kernel, alpha=alpha, chunk=chunk), grid=(t // tm,),
        in_specs=[pl.BlockSpec((n_grp, tm // chunk, cw), lambda i: (0, i, 0)), _tok_spec(tm, D_MODEL),
                  _tok_spec(tm, D_MODEL), pl.BlockSpec((None, tm, ple), lambda i: (layer, i, 0)),
                  _full_spec((1, D_MODEL)), _full_spec((1, D_MODEL)), _full_spec((D_MODEL, D_MODEL)),
                  _full_spec((D_MODEL, D_MODEL)), _full_spec((D_MODEL, D_MODEL)), _full_spec((ple, D_MODEL))],
        scratch_shapes=[pltpu.VMEM((D_MODEL // LANES, tm, LANES), F32),
                        pltpu.VMEM((DEINT_STAGE, tm // DEINT_STAGE, LANES), F32)],
        name="glu_post", **_post_outputs(t, tm),
    )(yg, z, x, p, ln_g, ln_b, w_glu, w_out, w_pg, w_pe)


def _comb_post(os_, lses, dils, z, z_col, x, p, layer, ln_g, ln_b, w_out, w_pg, w_pe, alpha):
    t = x.shape[0]
    tm = _token_tile(t, 512)
    ple = p.shape[2]
    if len(dils) == 1:
        group_specs = [_tok_spec(tm, D_MODEL)]
        scratch = []
    else:
        group_specs = [pl.BlockSpec((d, tm // d, D_MODEL), lambda i: (0, i, 0)) for d in dils]
        group_specs += [pl.BlockSpec((d, tm // d, LANES), lambda i: (0, i, 0)) for d in dils]
        scratch = [pltpu.VMEM((1, tm, LANES), F32) for d in dils if d > 1]
        scratch += [pltpu.VMEM((D_MODEL // LANES, tm, LANES), F32) for d in dils if d > 1]
    return pl.pallas_call(
        functools.partial(_comb_post_kernel, dils=tuple(dils), alpha=alpha), grid=(t // tm,),
        in_specs=group_specs
        + [_tok_spec(tm, D_MODEL, z_col), _tok_spec(tm, D_MODEL), pl.BlockSpec((None, tm, ple), lambda i: (layer, i, 0)),
           _full_spec((1, D_MODEL)), _full_spec((1, D_MODEL)), _full_spec((D_MODEL, D_MODEL)),
           _full_spec((D_MODEL, D_MODEL)), _full_spec((ple, D_MODEL))],
        scratch_shapes=scratch, name="attn_out_post", **_post_outputs(t, tm),
    )(*os_, *lses, z, x, p, ln_g, ln_b, w_out, w_pg, w_pe)


def _attn_prompt_kernel(q_ref, kp_ref, kc_ref, vp_ref, vc_ref, o_ref, lse_ref):
    ib = pl.program_id(2)
    tq = q_ref.shape[0]
    row = lax.broadcasted_iota(jnp.int32, (tq, 2 * tq), 0)
    col = lax.broadcasted_iota(jnp.int32, (tq, 2 * tq), 1)
    back = row + tq - col
    valid = (back >= 0) & (back <= WINDOW_KEYS) & ((col >= tq) | (ib > 0))
    lane = lax.broadcasted_iota(jnp.int32, (tq, LANES), 1)
    low = lane < HEAD_DIM
    head_of_lane = lane >> ((LANES // N_HEADS).bit_length() - 1)
    lse_tile = jnp.zeros((tq, LANES), F32)
    nt = (((1,), (1,)), ((), ()))
    for j in range(ATTN_WIDTH // LANES):
        sl = slice(j * LANES, (j + 1) * LANES)
        q2 = q_ref[:, sl]
        k2 = jnp.concatenate([kp_ref[:, sl], kc_ref[:, sl]], axis=0)
        v2 = jnp.concatenate([vp_ref[:, sl], vc_ref[:, sl]], axis=0)
        outs = []
        for sel in (low, jnp.logical_not(low)):
            qm = jnp.where(sel, q2, jnp.zeros_like(q2))
            s = lax.dot_general(qm, k2, nt, preferred_element_type=F32)
            s = jnp.where(valid, s, NEG_INF)
            m = jnp.max(s, axis=1, keepdims=True)
            pr = jnp.exp(s - m)
            l = jnp.sum(pr, axis=1, keepdims=True)
            o = jnp.dot(pr.astype(BF16), v2, preferred_element_type=F32)
            outs.append((o / l, m + jnp.log(l)))
        o_ref[:, sl] = jnp.where(low, outs[0][0], outs[1][0]).astype(o_ref.dtype)
        for hh in range(2):
            lse_tile = jnp.where(head_of_lane == 2 * j + hh, outs[hh][1], lse_tile)
    lse_ref[...] = lse_tile


def _attn_prompt_group(q, k, v, n_seq, seq_len):
    dil, rows, _ = q.shape
    tq = WINDOW_KEYS
    nb = seq_len // dil // tq
    assert rows == n_seq * nb * tq

    def cur(b, r, ib):
        return (r, b * nb + ib, 0)

    def prev(b, r, ib):
        return (r, b * nb + jnp.maximum(ib - 1, 0), 0)

    blk = (None, tq, D_MODEL)
    return pl.pallas_call(
        _attn_prompt_kernel, grid=(n_seq, dil, nb),
        in_specs=[pl.BlockSpec(blk, cur), pl.BlockSpec(blk, prev), pl.BlockSpec(blk, cur),
                  pl.BlockSpec(blk, prev), pl.BlockSpec(blk, cur)],
        out_specs=[pl.BlockSpec(blk, cur), pl.BlockSpec((None, tq, LANES), cur)],
        out_shape=[jax.ShapeDtypeStruct((dil, rows, D_MODEL), BF16), jax.ShapeDtypeStruct((dil, rows, LANES), F32)],
        compiler_params=pltpu.CompilerParams(dimension_semantics=("parallel", "parallel", "arbitrary"),
                                             vmem_limit_bytes=VMEM_LIMIT),
        name=f"attn_prompt_d{dil}",
    )(q, k, k, v, v)


SAMPLE_KEY_CHUNK = 512


def _attn_sample_kernel(q_ref, kvn_ref, c0_ref, c1_ref, c2_ref, o_ref, s_scr, m_scr):
    step = pl.program_id(1)
    ds = q_ref.shape[1]
    n_rows = N_HEADS * ds
    assert ds & (ds - 1) == 0 and n_rows == LANES
    rho = lax.broadcasted_iota(jnp.int32, (n_rows, ATTN_WIDTH), 0)
    lane = lax.broadcasted_iota(jnp.int32, (n_rows, ATTN_WIDTH), 1)
    head_mask = (rho >> (ds.bit_length() - 1)) == (lane >> (HEAD_DIM.bit_length() - 1))
    cache_refs = (c0_ref, c1_ref, c2_ref)
    nt = (((1,), (1,)), ((), ()))
    kvn = kvn_ref[0].astype(F32)

    def load_chunk(g, c0, cw):
        return cache_refs[g][0, 0, :, c0:c0 + cw].astype(BF16)

    def cache_valid(g, c0, cw):
        qi = lax.broadcasted_iota(jnp.int32, (n_rows, cw), 0) & (ds - 1)
        back = lax.broadcasted_iota(jnp.int32, (n_rows, cw), 1) + c0 - qi
        return (back >= 0) & ((back & (DILATIONS[g] - 1)) == 0)

    def new_block(g, off):
        piece = kvn[:, off + g * ATTN_WIDTH: off + (g + 1) * ATTN_WIDTH]
        pad = jnp.zeros((LANES - ds, ATTN_WIDTH), F32)
        return jnp.concatenate([piece, pad], axis=0).astype(BF16)

    chunks = []
    col = 0
    for g, c_ref in enumerate(cache_refs):
        width = c_ref.shape[3]
        for c0 in range(0, width, SAMPLE_KEY_CHUNK):
            cw = min(SAMPLE_KEY_CHUNK, width - c0)
            chunks.append((g, c0, cw, col))
            col += cw
    new_cols = [col + g * LANES for g in range(N_ATT_GROUPS)]

    @pl.when(step == 0)
    def _():
        q_all = q_ref[0].astype(F32)
        q_rows = []
        for g in range(N_ATT_GROUPS):
            qg = q_all[:, g * ATTN_WIDTH:(g + 1) * ATTN_WIDTH]
            q_rows.append(jnp.where(head_mask, jnp.concatenate([qg] * N_HEADS, axis=0), 0.0).astype(BF16))
        m = jnp.full((n_rows, 1), NEG_INF, F32)
        for g, c0, cw, col0 in chunks:
            s = jnp.dot(q_rows[g], load_chunk(g, c0, cw), preferred_element_type=F32)
            s = jnp.where(cache_valid(g, c0, cw), s, NEG_INF)
            s_scr[:, col0:col0 + cw] = s
            m = jnp.maximum(m, jnp.max(s, axis=1, keepdims=True))
        qi = lax.broadcasted_iota(jnp.int32, (n_rows, LANES), 0) & (ds - 1)
        kappa = lax.broadcasted_iota(jnp.int32, (n_rows, LANES), 1)
        for g in range(N_ATT_GROUPS):
            s = lax.dot_general(q_rows[g], new_block(g, 0), nt, preferred_element_type=F32)
            ahead = qi - kappa
            s = jnp.where((ahead >= 0) & ((ahead & (DILATIONS[g] - 1)) == 0) & (kappa < ds), s, NEG_INF)
            s_scr[:, new_cols[g]:new_cols[g] + LANES] = s
            m = jnp.maximum(m, jnp.max(s, axis=1, keepdims=True))
        m_scr[...] = jnp.broadcast_to(m, m_scr.shape)

    @pl.when(step == 1)
    def _():
        m = m_scr[:, 0:1]
        l = jnp.zeros((n_rows, 1), F32)
        acc = jnp.zeros((n_rows, ATTN_WIDTH), F32)
        for g, c0, cw, col0 in chunks:
            pr = jnp.exp(s_scr[:, col0:col0 + cw] - m)
            l = l + jnp.sum(pr, axis=1, keepdims=True)
            acc = acc + lax.dot_general(pr.astype(BF16), load_chunk(g, c0, cw), nt, preferred_element_type=F32)
        for g in range(N_ATT_GROUPS):
            pr = jnp.exp(s_scr[:, new_cols[g]:new_cols[g] + LANES] - m)
            l = l + jnp.sum(pr, axis=1, keepdims=True)
            acc = acc + jnp.dot(pr.astype(BF16), new_block(g, N_ATT_GROUPS * ATTN_WIDTH),
                                preferred_element_type=F32)
        acc = acc / l
        acc3 = acc.reshape(N_HEADS, ds, ATTN_WIDTH)
        low = lax.broadcasted_iota(jnp.int32, (ds, LANES), 1) < HEAD_DIM
        for c in range(ATTN_WIDTH // LANES):
            sl = slice(c * LANES, (c + 1) * LANES)
            o_ref[0, :, sl] = jnp.where(low, acc3[2 * c][:, sl], acc3[2 * c + 1][:, sl])


def _cache_views(caches):
    views = []
    for cache, dil in zip(caches, DILATIONS):
        n, width = cache.shape[:2]
        assert width == WINDOW_KEYS * dil and cache.shape[2:] == (2, N_HEADS, HEAD_DIM)
        views.append(cache.transpose(0, 2, 3, 4, 1).reshape(n, 2, ATTN_WIDTH, width))
    return views


def _attn_sample(qz_s, kv_s, views, n_seq, ds):
    q3 = qz_s.reshape(n_seq, ds, qz_s.shape[1])
    kv3 = kv_s.reshape(n_seq, ds, kv_s.shape[1])
    n_cols = sum(v.shape[3] for v in views) + N_ATT_GROUPS * LANES

    def tok_spec(width):
        return pl.BlockSpec((1, ds, width), lambda n, s: (n, 0, 0))

    def cache_spec(width):
        return pl.BlockSpec((1, 1, ATTN_WIDTH, width), lambda n, s: (n, s, 0, 0))

    o = pl.pallas_call(
        _attn_sample_kernel, grid=(n_seq, 2),
        in_specs=[tok_spec(q3.shape[2]), tok_spec(kv3.shape[2])] + [cache_spec(v.shape[3]) for v in views],
        out_specs=tok_spec(ATTN_WIDTH), out_shape=jax.ShapeDtypeStruct((n_seq, ds, ATTN_WIDTH), F32),
        scratch_shapes=[pltpu.VMEM((N_HEADS * ds, n_cols), F32), pltpu.VMEM((N_HEADS * ds, LANES), F32)],
        compiler_params=pltpu.CompilerParams(dimension_semantics=("parallel", "arbitrary"),
                                             vmem_limit_bytes=VMEM_LIMIT),
        name="attn_sample",
    )(q3, kv3, *views)
    return o.reshape(n_seq * ds, ATTN_WIDTH)


def kernel(x_prompt, x_sample, state_ssm_re, state_ssm_im, cache_kv_w128, cache_kv_w512, cache_kv_w2048, p_prompt, p_sample, ln_g, ln_b, w_pe, w_pg, w_in_a, a_re, a_im, log_dt, b_re, b_im, c_re, c_im, d_skip, w_glu, w_out_a, w_kv, w_in_b, w_out_b):
    depth = ln_g.shape[0]
    n_a = w_in_a.shape[0]
    batch, seq, _ = x_prompt.shape
    n_s, ds, _ = x_sample.shape
    past = cache_kv_w2048.shape[1]
    assert ds * 2 == SSM_CHUNK and ds <= DILATIONS[2] and past == WINDOWS[2] and seq % (DILATIONS[2] * WINDOW_KEYS) == 0
    alpha = (2 * depth) ** 0.25
    t_p, t_s = batch * seq, n_s * ds
    caches =(cache_kv_w128, cache_kv_w512, cache_kv_w2048)

    xs = [x_prompt.reshape(t_p, D_MODEL), x_sample.reshape(t_s, D_MODEL)]
    xbs = list(xs)
    ps = [p_prompt.reshape(depth, t_p, -1), p_sample.reshape(depth, t_s, -1)]
    wb = lambda w: w.astype(BF16)
    h_prompt, h_sample = [], []

    for i in range(n_a):
        w_in = wb(w_in_a[i])
        (xg, z_p), (xsg, z_s) = _project_ssm(xbs[0], w_in, SSM_CHUNK), _project_ssm(xbs[1], w_in, ds)
        ops = _ssm_operators(a_re[i], a_im[i], log_dt[i], b_re[i], b_im[i], c_re[i], c_im[i], d_skip[i])
        h0g = jnp.concatenate([state_ssm_re[i], state_ssm_im[i]], axis=-1).astype(F32).transpose(1, 0, 2)
        yg, hl, ysg, hs = _ssm(xg, xsg, h0g, ops, n_seq=batch)
        h_prompt.append(hl.transpose(1, 0, 2))
        h_sample.append(hs.transpose(1, 0, 2))
        for k, (y, z) in enumerate(((yg, z_p), (ysg, z_s))):
            xs[k], xbs[k] = _glu_post(y, z, xs[k], ps[k], i, ln_g[i:i + 1], ln_b[i:i + 1], wb(w_glu[i]),
                                      wb(w_out_a[i]), wb(w_pg[i]), wb(w_pe[i]), alpha)

    pos = [jnp.arange(seq, dtype=F32), jnp.tile(past + jnp.arange(ds, dtype=F32), n_s)]
    rope_k = [_rope_tables(p_, 1.0) for p_ in pos]
    rope_q = [tuple(HEAD_DIM ** -0.5 * tab for tab in tabs) for tabs in rope_k]
    q_width = N_ATT_GROUPS * ATTN_WIDTH
    n_grp = N_ATT_GROUPS
    w_kv_b = wb(w_kv)
    k_p = [_project_deint(xbs[0], w_kv_b, grp, DILATIONS[grp], rope_k[0]) for grp in range(n_grp)]
    v_p = [_project_deint(xbs[0], w_kv_b, n_grp + grp, DILATIONS[grp]) for grp in range(n_grp)]
    kv_s = _project(xbs[1], w_kv_b, rope=rope_k[1], rot_cols=q_width, tn=512)
    views = _cache_views(caches)

    for i in range(n_a, depth):
        j = i - n_a
        w_in = wb(w_in_b[j])
        post_w = (ln_g[i:i + 1], ln_b[i:i + 1], wb(w_out_b[j]), wb(w_pg[i]), wb(w_pe[i]), alpha)
        q_p = [_project_deint(xbs[0], w_in, grp, DILATIONS[grp], rope_q[0]) for grp in range(n_grp)]
        z_p = _project_deint(xbs[0], w_in, n_grp, 1).reshape(t_p, D_MODEL)
        parts = [_attn_prompt_group(q_p[grp], k_p[grp], v_p[grp], batch, seq) for grp in range(n_grp)]
        qz_s = _project(xbs[1], w_in, rope=rope_q[1], rot_cols=q_width, tn=512)
        o_s = _attn_sample(qz_s, kv_s, views, n_s, ds)
        xs[0], xbs[0] = _comb_post([pt[0] for pt in parts], [pt[1] for pt in parts], DILATIONS, z_p, 0,
                                   xs[0], ps[0], i, *post_w)
        xs[1], xbs[1] = _comb_post([o_s], [], (1,), qz_s, n_grp, xs[1], ps[1], i, *post_w)

    y_prompt = xs[0].reshape(batch, seq, D_MODEL)
    y_sample = xs[1].reshape(n_s, ds, D_MODEL)
    hp = jnp.stack(h_prompt)
    hsm = jnp.stack(h_sample)

    def prompt_window(arr, dil, width):
        per = arr.reshape(dil, batch, seq // dil, ATTN_WIDTH)[:, :, (seq - width) // dil:]
        return per.transpose(1, 2, 0, 3).reshape(batch, width, N_HEADS, HEAD_DIM).astype(F32)

    kv_prompt = [jnp.stack([prompt_window(k_p[grp], DILATIONS[grp], min(w, seq)),
                            prompt_window(v_p[grp], DILATIONS[grp], min(w, seq))], axis=2)
                 for grp, w in enumerate(WINDOWS)]
    kv_s6 = kv_s.astype(F32).reshape(n_s, ds, 2, N_ATT_GROUPS, N_HEADS, HEAD_DIM)
    kv_sample = [kv_s6[:, :, :, grp] for grp in range(N_ATT_GROUPS)]
    return (y_prompt, y_sample, hp[..., :SSM_STATE], hp[..., SSM_STATE:], hsm[..., :SSM_STATE], hsm[..., SSM_STATE:],
            *kv_prompt, *kv_sample)
```

```python
import functools
import math

import jax
import jax.numpy as jnp
from jax import lax
from jax.experimental import pallas as pl
from jax.experimental.pallas import tpu as pltpu

F32 = jnp.float32
BF16 = jnp.bfloat16

D_MODEL = 1024
N_SSM_GROUPS = 64
SSM_GROUP = 16
SSM_STATE = 64
SSM_CHUNK = 16
N_ATT_GROUPS = 3
WINDOWS = (128, 512, 2048)
DILATIONS = (1, 4, 16)
HEAD_DIM = 64
N_HEADS = 16
ATTN_WIDTH = N_HEADS * HEAD_DIM
ROT_DIM = HEAD_DIM // 4
ROPE_THETA = 500000.0
WINDOW_KEYS = 128
LN_EPS = 1e-5
NEG_INF = -1e30
LANES = 128
SUBLANES = 8
VMEM_LIMIT = 52 * 1024 * 1024


def _sigmoid(x):
    return 1.0 / (1.0 + jnp.exp(-x))


def _gelu_tanh(x):
    return 0.5 * x * (1.0 + jnp.tanh(math.sqrt(2.0 / math.pi) * (x + 0.044715 * (x * x * x))))


def _mm_kernel(x_ref, w_ref, o_ref):
    o_ref[...] = jnp.dot(x_ref[...], w_ref[...], preferred_element_type=F32).astype(o_ref.dtype)


def _mm_rope_kernel(x_ref, w_ref, cos_ref, sa_ref, sb_ref, o_ref, *, rot_blocks):
    j = pl.program_id(1)
    acc = jnp.dot(x_ref[...], w_ref[...], preferred_element_type=F32)

    @pl.when(j < rot_blocks)
    def _():
        o_ref[...] = _rope(acc, cos_ref[...], sa_ref[...], sb_ref[...]).astype(o_ref.dtype)

    @pl.when(j >= rot_blocks)
    def _():
        o_ref[...] = acc.astype(o_ref.dtype)


def _token_tile(n_tokens, want):
    tm = min(want, n_tokens)
    assert n_tokens % tm == 0
    return tm


def _project(x, w, *, rope=None, rot_cols=0, tn=512, out_dtype=BF16):
    t, k = x.shape
    n = w.shape[1]
    tm = _token_tile(t, 1024)
    grid = (t // tm, n // tn)
    x_spec = pl.BlockSpec((tm, k), lambda i, j: (i, 0))
    w_spec = pl.BlockSpec((k, tn), lambda i, j: (0, j))
    o_spec = pl.BlockSpec((tm, tn), lambda i, j: (i, j))
    params = pltpu.CompilerParams(dimension_semantics=("parallel", "arbitrary"), vmem_limit_bytes=VMEM_LIMIT)
    if rope is None:
        return pl.pallas_call(
            _mm_kernel, grid=grid, in_specs=[x_spec, w_spec], out_specs=o_spec,
            out_shape=jax.ShapeDtypeStruct((t, n), out_dtype), compiler_params=params, name="proj",
        )(x, w)
    assert rot_cols % tn == 0
    tab_spec = pl.BlockSpec((tm, LANES), lambda i, j: (i, 0))
    return pl.pallas_call(
        functools.partial(_mm_rope_kernel, rot_blocks=rot_cols // tn), grid=grid,
        in_specs=[x_spec, w_spec, tab_spec, tab_spec, tab_spec], out_specs=o_spec,
        out_shape=jax.ShapeDtypeStruct((t, n), out_dtype), compiler_params=params, name="proj_rope",
    )(x, w, *rope)


PROJ_SUB_ROWS = 512
DEINT_STAGE = 4


def _rope(acc, cos, sa, sb):
    half = ROT_DIM // 2
    outs = []
    for c in range(acc.shape[1] // LANES):
        a = acc[:, c * LANES:(c + 1) * LANES]
        outs.append(a * cos + pltpu.roll(a, LANES - half, 1) * sa + pltpu.roll(a, half, 1) * sb)
    return jnp.concatenate(outs, axis=1)


def _mm_deint_kernel(*refs, dil, rope):
    if rope:
        x_ref, w_ref, cos_ref, sa_ref, sb_ref, o_ref = refs[:6]
    else:
        x_ref, w_ref, o_ref = refs[:3]
    tm = x_ref.shape[0]
    sub = min(tm, PROJ_SUB_ROWS)
    for b in range(tm // sub):
        rs = slice(b * sub, (b + 1) * sub)
        acc = jnp.dot(x_ref[rs, :], w_ref[...], preferred_element_type=F32)
        if rope:
            acc = _rope(acc, cos_ref[rs, :], sa_ref[rs, :], sb_ref[rs, :])
        if dil == 1:
            o_ref[0, rs, :] = acc.astype(o_ref.dtype)
            continue
        acc_scr, mid_scr = refs[-2:]
        rows = sub // dil
        f1 = min(dil, DEINT_STAGE)
        f2 = dil // f1
        for c in range(acc.shape[1] // LANES):
            sl = slice(c * LANES, (c + 1) * LANES)
            acc_scr[c, rs, :] = acc[:, sl]
            for r1 in range(f1):
                if f2 == 1:
                    o_ref[r1, b * rows:(b + 1) * rows, sl] = (
                        acc_scr[c, pl.ds(b * sub + r1, rows, stride=dil), :].astype(o_ref.dtype))
                    continue
                mid_scr[r1] = acc_scr[c, pl.ds(b * sub + r1, sub // f1, stride=f1), :]
                for r2 in range(f2):
                    o_ref[r1 + f1 * r2, b * rows:(b + 1) * rows, sl] = (
                        mid_scr[r1, pl.ds(r2, rows, stride=f2), :].astype(o_ref.dtype))


def _project_deint(x, w, col, dil, rope=None):
    t, k = x.shape
    tm = _token_tile(t, 1024)
    wn = D_MODEL
    in_specs = [pl.BlockSpec((tm, k), lambda i: (i, 0)), pl.BlockSpec((k, wn), lambda i: (0, col))]
    args = [x, w]
    if rope is not None:
        tab_blocks = rope[0].shape[0] // tm
        in_specs += [pl.BlockSpec((tm, LANES), lambda i: (i % tab_blocks, 0))] * 3
        args += list(rope)
    return pl.pallas_call(
        functools.partial(_mm_deint_kernel, dil=dil, rope=rope is not None), grid=(t // tm,),
        in_specs=in_specs, out_specs=pl.BlockSpec((dil, tm // dil, wn), lambda i: (0, i, 0)),
        out_shape=jax.ShapeDtypeStruct((dil, t // dil, wn), BF16),
        scratch_shapes=[pltpu.VMEM((wn // LANES, tm, LANES), F32),
                        pltpu.VMEM((DEINT_STAGE, min(tm, PROJ_SUB_ROWS) // DEINT_STAGE, LANES), F32)] if dil > 1 else [],
        compiler_params=pltpu.CompilerParams(dimension_semantics=("parallel",), vmem_limit_bytes=VMEM_LIMIT),
        name=f"proj_deint{dil}",
    )(*args)


GROUPS_PER_SLAB = LANES // SSM_GROUP


def _slot_position(lanes_shape):
    return lax.broadcasted_iota(jnp.int32, lanes_shape, 1) >> (SSM_GROUP.bit_length() - 1)


def _mm_ssm_kernel(x_ref, w_ref, xg_ref, z_ref, slab_scr, mid_scr, *, chunk):
    tm = x_ref.shape[0]
    n_chunks = tm // chunk
    x = x_ref[...].astype(BF16)
    z_ref[...] = jnp.dot(x, w_ref[:, D_MODEL:], preferred_element_type=F32).astype(z_ref.dtype)
    u = jnp.dot(x, w_ref[:, :D_MODEL], preferred_element_type=F32)
    slot = _slot_position((n_chunks, LANES))
    gps = GROUPS_PER_SLAB
    f1 = DEINT_STAGE
    f2 = chunk // f1
    for c in range(D_MODEL // LANES):
        slab_scr[c] = u[:, c * LANES:(c + 1) * LANES]
        for q in range(f1):
            mid_scr[q] = slab_scr[c, pl.ds(q, tm // f1, stride=f1), :]
        for j in range(chunk // gps):
            rot = []
            for r in range(gps):
                s = gps * j + r
                rows = mid_scr[s % f1, pl.ds(s // f1, n_chunks, stride=f2), :]
                rot.append(rows if r == 0 else pltpu.roll(rows, r * SSM_GROUP, 1))
            for gg in range(gps):
                tile = rot[0]
                for r in range(1, gps):
                    tile = jnp.where(slot == (gg + r) % gps, rot[r], tile)
                xg_ref[c * gps + gg, :, j * LANES:(j + 1) * LANES] = tile.astype(xg_ref.dtype)


def _project_ssm(x, w, chunk):
    t, k = x.shape
    tm = _token_tile(t, 512)
    cw = chunk * SSM_GROUP
    return pl.pallas_call(
        functools.partial(_mm_ssm_kernel, chunk=chunk), grid=(t // tm,),
        in_specs=[pl.BlockSpec((tm, k), lambda i: (i, 0)), pl.BlockSpec(w.shape, lambda i: (0, 0))],
        out_specs=[pl.BlockSpec((N_SSM_GROUPS, tm // chunk, cw), lambda i: (0, i, 0)),
                   pl.BlockSpec((tm, D_MODEL), lambda i: (i, 0))],
        out_shape=[jax.ShapeDtypeStruct((N_SSM_GROUPS, t // chunk, cw), BF16),
                   jax.ShapeDtypeStruct((t, D_MODEL), BF16)],
        scratch_shapes=[pltpu.VMEM((D_MODEL // LANES, tm, LANES), F32),
                        pltpu.VMEM((DEINT_STAGE, tm // DEINT_STAGE, LANES), F32)],
        compiler_params=pltpu.CompilerParams(dimension_semantics=("parallel",), vmem_limit_bytes=VMEM_LIMIT),
        name=f"proj_ssm{chunk}",
    )(x, w)


def _rope_tables(pos, scale):
    half = ROT_DIM // 2
    inv_freq = ROPE_THETA ** (-jnp.arange(0, ROT_DIM, 2, dtype=F32) / ROT_DIM)
    ang = pos[:, None] * inv_freq[None, :]
    cos, sin = jnp.cos(ang), jnp.sin(ang)
    t = pos.shape[0]
    ones = jnp.ones((t, HEAD_DIM - ROT_DIM), F32)
    zeros = jnp.zeros((t, HEAD_DIM - ROT_DIM), F32)
    zhalf = jnp.zeros((t, half), F32)
    cos64 = jnp.concatenate([cos, cos, ones], axis=1)
    sa64 = jnp.concatenate([-sin, zhalf, zeros], axis=1)
    sb64 = jnp.concatenate([zhalf, sin, zeros], axis=1)
    return tuple(scale * jnp.concatenate([a, a], axis=1) for a in (cos64, sa64, sb64))


COEF_BLOCK_STEPS = tuple(SSM_CHUNK << k for k in range(3))
COEF_CARRY_STEPS = tuple(SSM_CHUNK * SUBLANES << k for k in range(6))
COEF_HALF_ROW = len(COEF_BLOCK_STEPS) + len(COEF_CARRY_STEPS)
COEF_IN_BLOCK_ROW = 16
N_COEF_ROWS = COEF_IN_BLOCK_ROW + SUBLANES


def _cpair_mul(a, b):
    return a[0] * b[0] - a[1] * b[1], a[0] * b[1] + a[1] * b[0]


def _abar_and_factor(a_re, a_im, dt):
    mag = jnp.exp(dt * a_re)
    ab = (mag * jnp.cos(dt * a_im), mag * jnp.sin(dt * a_im))
    den = a_re * a_re + a_im * a_im
    nr, ni = ab[0] - 1.0, ab[1]
    fac = ((nr * a_re + ni * a_im) / den, (ni * a_re - nr * a_im) / den)
    return ab, fac


def _powers(ab, n):
    out = [(jnp.ones_like(ab[0]), jnp.zeros_like(ab[0]))]
    for _ in range(n):
        out.append(_cpair_mul(out[-1], ab))
    return out


OPS_GROUPS_PER_STEP = 1


def _ssm_ops_kernel(*refs):
    io_refs, (rhs_scr, t_scr) = refs[:-2], refs[-2:]
    for j in range(OPS_GROUPS_PER_STEP):
        group = pl.program_id(0) * OPS_GROUPS_PER_STEP + j
        _ssm_ops_group(group, *[r.at[pl.ds(j, 1)] for r in io_refs], rhs_scr.at[j], t_scr.at[j])


def _roll_lane_tiles(x, shift):
    tiles = [pltpu.roll(x[:, k * LANES:(k + 1) * LANES], shift, 1) for k in range(x.shape[1] // LANES)]
    return tiles[0] if len(tiles) == 1 else jnp.concatenate(tiles, axis=1)


def _ssm_ops_group(group, arow_ref, bt_ref, c_ref, d_ref, t_ref, bin_ref, cout_ref, ca_ref, cb_ref,
                   rhs_scr, t_scr):
    c, m, p = SSM_CHUNK, SSM_GROUP, SSM_STATE
    gps = GROUPS_PER_SLAB
    hi = lax.Precision.HIGHEST
    rot = group & (gps - 1)

    def slot_row(s):
        return pl.multiple_of((s // gps * gps + ((rot + s % gps) & (gps - 1))) * m, m)

    ab, fac = _abar_and_factor(arow_ref[0, 0:1, :], arow_ref[0, 1:2, :], arow_ref[0, 2:3, :])
    bbar = _cpair_mul((bt_ref[0, 0], bt_ref[0, 1]), fac)
    pw = _powers(ab, c)
    for s in range(c):
        blk = _cpair_mul(bbar, pw[c - 1 - s])
        bin_ref[0, pl.ds(slot_row(s), m), 0:p] = blk[0].astype(bin_ref.dtype)
        bin_ref[0, pl.ds(slot_row(s), m), p:2 * p] = blk[1].astype(bin_ref.dtype)
    coef = {}
    cur = pw[c]
    for step in COEF_BLOCK_STEPS + COEF_CARRY_STEPS:
        coef[step] = cur
        cur = _cpair_mul(cur, cur)
    rows = [coef[s] for s in COEF_BLOCK_STEPS + COEF_CARRY_STEPS] + [pw[c // 2]]
    rows += [pw[0]] * (COEF_IN_BLOCK_ROW - len(rows))
    in_block = [pw[0]]
    for _ in range(SUBLANES - 1):
        in_block.append(_cpair_mul(in_block[-1], pw[c]))
    rows += in_block
    for k, (re, im) in enumerate(rows):
        ca_ref[0, k:k + 1, 0:p] = re
        ca_ref[0, k:k + 1, p:2 * p] = re
        cb_ref[0, k:k + 1, 0:p] = -im
        cb_ref[0, k:k + 1, p:2 * p] = im

    cm = (c_ref[0, 0], c_ref[0, 1])
    for t in range(c + 1):
        blk = _cpair_mul(cm, pw[t])
        if t < c:
            rhs_scr[0, t * m:(t + 1) * m, :] = blk[0]
            rhs_scr[1, t * m:(t + 1) * m, :] = blk[1]
        if t >= 1:
            cout_ref[0, pl.ds(slot_row(t - 1), m), 0:p] = blk[0].astype(cout_ref.dtype)
            cout_ref[0, pl.ds(slot_row(t - 1), m), p:2 * p] = (-blk[1]).astype(cout_ref.dtype)
    lane_shift = rot * m

    nt = (((1,), (1,)), ((), ()))
    kern = (lax.dot_general(bbar[0], rhs_scr[0], nt, precision=hi, preferred_element_type=F32)
            - lax.dot_general(bbar[1], rhs_scr[1], nt, precision=hi, preferred_element_type=F32))
    lane = lax.broadcasted_iota(jnp.int32, kern.shape, 1)
    row = lax.broadcasted_iota(jnp.int32, kern.shape, 0)
    for s in range(c):
        shifted = kern if s == 0 else jnp.where(lane >= s * m, pltpu.roll(kern, s * m, 1), 0.0)
        shifted = shifted + jnp.where(lane == row + s * m, d_ref[0], 0.0)
        t_scr[pl.ds(slot_row(s), m), :] = shifted
    t_ref[0] = _roll_lane_tiles(t_scr[...], lane_shift).astype(t_ref.dtype)


def _ssm_operators(a_re, a_im, log_dt, b_re, b_im, c_re, c_im, d_skip):
    g, p, m, c = N_SSM_GROUPS, SSM_STATE, SSM_GROUP, SSM_CHUNK
    dt = jnp.broadcast_to(jnp.exp(log_dt.astype(F32))[:, None], (g, p))
    arow = jnp.stack([a_re.astype(F32), a_im.astype(F32), dt], axis=1)
    bt = jnp.stack([b_re, b_im], axis=1).astype(F32).transpose(0, 1, 3, 2)
    cm = jnp.stack([c_re, c_im], axis=1).astype(F32)
    d_rep = jnp.tile(d_skip.astype(F32).reshape(g, 1, m), (1, 1, c))
    cw = c * m

    gps = OPS_GROUPS_PER_STEP

    def spec(*shape):
        return pl.BlockSpec((gps,) + shape, lambda i: (i,) + (0,) * len(shape))

    return pl.pallas_call(
        _ssm_ops_kernel, grid=(g // gps,),
        in_specs=[spec(3, p), spec(2, m, p), spec(2, m, p), spec(1, cw)],
        out_specs=[spec(cw, cw), spec(cw, 2 * p), spec(cw, 2 * p), spec(N_COEF_ROWS, 2 * p), spec(N_COEF_ROWS, 2 * p)],
        out_shape=[jax.ShapeDtypeStruct((g, cw, cw), BF16), jax.ShapeDtypeStruct((g, cw, 2 * p), BF16),
                   jax.ShapeDtypeStruct((g, cw, 2 * p), BF16), jax.ShapeDtypeStruct((g, N_COEF_ROWS, 2 * p), F32),
                   jax.ShapeDtypeStruct((g, N_COEF_ROWS, 2 * p), F32)],
        scratch_shapes=[pltpu.VMEM((gps, 2, cw, p), F32), pltpu.VMEM((gps, cw, cw), F32)],
        compiler_params=pltpu.CompilerParams(dimension_semantics=("parallel",)),
        name="s5_operators",
    )(arow, bt, cm, d_rep)


def _swap_halves(x):
    return pltpu.roll(x, SSM_STATE, x.ndim - 1)


def _cmul(x, xs, ca, cb):
    return x * ca + xs * cb


def _ssm_kernel(x_ref, xs_ref, h0_ref, t_ref, bin_ref, cout_ref, ca_ref, cb_ref,
                y_ref, hl_ref, ys_ref, hs_ref, l_scr, e_scr, *, n_seq, rows_per_seq):
    rows = n_seq * rows_per_seq
    n_blocks = rows_per_seq // SUBLANES
    ca = ca_ref[0]
    cb = cb_ref[0]
    x = x_ref[0]
    v = jnp.dot(x, bin_ref[0], preferred_element_type=F32)
    rmod = lax.broadcasted_iota(jnp.int32, (rows, LANES), 0) & (SUBLANES - 1)

    for k, dist in enumerate((1, 2, 4)):
        sh = jnp.where(rmod >= dist, pltpu.roll(v, dist, 0), 0.0)
        v = v + _cmul(sh, _swap_halves(sh), ca[k:k + 1], cb[k:k + 1])
    l_scr[...] = v

    assert n_blocks & (n_blocks - 1) == 0 and n_blocks <= 1 << len(COEF_CARRY_STEPS)
    e = l_scr[pl.ds(SUBLANES - 1, rows // SUBLANES, stride=SUBLANES), :]
    bmod = lax.broadcasted_iota(jnp.int32, e.shape, 0) & (n_blocks - 1)
    for k in range(n_blocks.bit_length() - 1):
        row = len(COEF_BLOCK_STEPS) + k
        sh = jnp.where(bmod >= (1 << k), pltpu.roll(e, 1 << k, 0), 0.0)
        e = e + _cmul(sh, _swap_halves(sh), ca[row:row + 1], cb[row:row + 1])
    hl_ref[0] = jnp.concatenate([e[(s + 1) * n_blocks - 1:(s + 1) * n_blocks] for s in range(n_seq)], axis=0)
    e_prev = jnp.where(bmod >= 1, pltpu.roll(e, 1, 0), 0.0)
    for r in range(SUBLANES):
        e_scr[pl.ds(r, rows // SUBLANES, stride=SUBLANES), :] = e_prev

    lshift = jnp.where(rmod >= 1, pltpu.roll(v, 1, 0), 0.0)
    ebc = e_scr[...]
    pwa = ca[COEF_IN_BLOCK_ROW:COEF_IN_BLOCK_ROW + SUBLANES]
    pwb = cb[COEF_IN_BLOCK_ROW:COEF_IN_BLOCK_ROW + SUBLANES]
    shape3 = (rows // SUBLANES, SUBLANES, LANES)
    h = (lshift.reshape(shape3) + ebc.reshape(shape3) * pwa[None]
         + _swap_halves(ebc).reshape(shape3) * pwb[None]).reshape(rows, LANES)
    nt = (((1,), (1,)), ((), ()))
    y = jnp.dot(x, t_ref[0], preferred_element_type=F32)
    y = y + lax.dot_general(h.astype(BF16), cout_ref[0], nt, preferred_element_type=F32)
    y_ref[0] = y.astype(y_ref.dtype)

    half = SSM_CHUNK * SSM_GROUP // 2
    xs = xs_ref[0]
    h0 = h0_ref[0]
    ys = jnp.dot(xs, t_ref[0, :half, :half], preferred_element_type=F32)
    ys = ys + lax.dot_general(h0.astype(BF16), cout_ref[0, :half, :], nt, preferred_element_type=F32)
    ys_ref[0] = ys.astype(ys_ref.dtype)
    hs = _cmul(h0, _swap_halves(h0), ca[COEF_HALF_ROW:COEF_HALF_ROW + 1], cb[COEF_HALF_ROW:COEF_HALF_ROW + 1])
    hs_ref[0] = hs + jnp.dot(xs, bin_ref[0, half:, :], preferred_element_type=F32)


def _ssm(xg, xsg, h0g, ops, *, n_seq):
    t_op, b_in, c_out, ca, cb = ops
    g, rows, cw = xg.shape
    n_s = xsg.shape[1]
    hw = xsg.shape[2]

    def spec(shape):
        return pl.BlockSpec((1,) + shape, lambda i: (i, 0, 0))

    return pl.pallas_call(
        functools.partial(_ssm_kernel, n_seq=n_seq, rows_per_seq=rows // n_seq),
        grid=(g,),
        in_specs=[spec((rows, cw)), spec((n_s, hw)), spec((n_s, LANES)), spec((cw, cw)), spec((cw, LANES)),
                  spec((cw, LANES)), spec((N_COEF_ROWS, LANES)), spec((N_COEF_ROWS, LANES))],
        out_specs=[spec((rows, cw)), spec((n_seq, LANES)), spec((n_s, hw)), spec((n_s, LANES))],
        out_shape=[jax.ShapeDtypeStruct((g, rows, cw), BF16), jax.ShapeDtypeStruct((g, n_seq, LANES), F32),
                   jax.ShapeDtypeStruct((g, n_s, hw), BF16), jax.ShapeDtypeStruct((g, n_s, LANES), F32)],
        scratch_shapes=[pltpu.VMEM((rows, LANES), F32)] * 2,
        compiler_params=pltpu.CompilerParams(dimension_semantics=("parallel",), vmem_limit_bytes=VMEM_LIMIT),
        name="s5_scan",
    )(xg, xsg, h0g, t_op, b_in, c_out, ca, cb)


def _post_layer(x, sub, p, g_ref, b_ref, wpg_ref, wpe_ref, alpha):
    r = alpha * x + sub
    mu = jnp.mean(r, axis=-1, keepdims=True)
    cen = r - mu
    var = jnp.mean(cen * cen, axis=-1, keepdims=True)
    h = cen * lax.rsqrt(var + LN_EPS) * g_ref[...] + b_ref[...]
    gate = _sigmoid(jnp.dot(h.astype(BF16), wpg_ref[...], preferred_element_type=F32))
    ple = jnp.dot(p.astype(BF16), wpe_ref[...], preferred_element_type=F32)
    return h + gate * ple


def _glu_post_kernel(yg_ref, z_ref, x_ref, p_ref, g_ref, b_ref, wglu_ref, wout_ref, wpg_ref, wpe_ref,
                     xo_ref, xb_ref, slab_scr, mid_scr, *, alpha, chunk):
    n_chunks = yg_ref.shape[1]
    slot = _slot_position((n_chunks, LANES))
    gps = GROUPS_PER_SLAB
    f1 = DEINT_STAGE
    f2 = chunk // f1
    tm = n_chunks * chunk
    for c in range(D_MODEL // LANES):
        for j in range(chunk // gps):
            tiles = [yg_ref[c * gps + gg, :, j * LANES:(j + 1) * LANES].astype(F32) for gg in range(gps)]
            for r in range(gps):
                picked = tiles[0]
                for gg in range(1, gps):
                    picked = jnp.where(slot == (gg + r) % gps, tiles[gg], picked)
                rows = picked if r == 0 else pltpu.roll(picked, LANES - r * SSM_GROUP, 1)
                t = gps * j + r
                mid_scr[t % f1, pl.ds(t // f1, n_chunks, stride=f2), :] = rows
        for q in range(f1):
            slab_scr[c, pl.ds(q, tm // f1, stride=f1), :] = mid_scr[q]
    y = jnp.concatenate([slab_scr[c] for c in range(D_MODEL // LANES)], axis=1)
    g = _gelu_tanh(y)
    t = jnp.dot(g.astype(BF16), wglu_ref[...], preferred_element_type=F32)
    z = z_ref[...].astype(F32)
    glu = g * _sigmoid(t) * (z * _sigmoid(z))
    sub = jnp.dot(glu.astype(BF16), wout_ref[...], preferred_element_type=F32)
    out = _post_layer(x_ref[...], sub, p_ref[...], g_ref, b_ref, wpg_ref, wpe_ref, alpha)
    xo_ref[...] = out
    xb_ref[...] = out.astype(BF16)


def _interleave_rows(ref, scr, dil):
    if dil == 1:
        return ref[0].astype(F32)
    rows = ref.shape[1]
    slabs = []
    for c in range(ref.shape[2] // LANES):
        for r in range(dil):
            scr[c, pl.ds(r, rows, stride=dil), :] = ref[r, :, c * LANES:(c + 1) * LANES].astype(F32)
        slabs.append(scr[c])
    return slabs[0] if len(slabs) == 1 else jnp.concatenate(slabs, axis=1)


def _comb_post_kernel(*refs, dils, alpha):
    n = len(dils)
    if n == 1:
        o_ref = refs[0]
        z_ref, x_ref, p_ref, g_ref, b_ref, wout_ref, wpg_ref, wpe_ref, xo_ref, xb_ref = refs[1:]
        o = o_ref[...].astype(F32)
    else:
        o_refs, lse_refs = refs[:n], refs[n:2 * n]
        z_ref, x_ref, p_ref, g_ref, b_ref, wout_ref, wpg_ref, wpe_ref, xo_ref, xb_ref = refs[2 * n:2 * n + 10]
        scrs = list(refs[2 * n + 10:])
        lses = [_interleave_rows(r, scrs.pop(0) if d > 1 else None, d) for r, d in zip(lse_refs, dils)]
        top = functools.reduce(jnp.maximum, lses)
        ws = [jnp.exp2(l - top) for l in lses]
        tot = functools.reduce(lambda a, c: a + c, ws)
        src = lax.broadcasted_iota(jnp.int32, (LANES, ATTN_WIDTH), 0)
        dst = lax.broadcasted_iota(jnp.int32, (LANES, ATTN_WIDTH), 1)
        lanes_per_head = LANES // N_HEADS
        expand = jnp.where(src == (dst >> (HEAD_DIM.bit_length() - 1)) * lanes_per_head, 1.0, 0.0).astype(BF16)
        o = None
        for w, r, d in zip(ws, o_refs, dils):
            a = w / tot
            hi = a.astype(BF16)
            lo = (a - hi.astype(F32)).astype(BF16)
            wide = (jnp.dot(hi, expand, preferred_element_type=F32) + jnp.dot(lo, expand, preferred_element_type=F32))
            term = wide * _interleave_rows(r, scrs.pop(0) if d > 1 else None, d)
            o = term if o is None else o + term
    z = z_ref[...].astype(F32)
    gated = o * (z * _sigmoid(z))
    sub = jnp.dot(gated.astype(BF16), wout_ref[...], preferred_element_type=F32)
    out = _post_layer(x_ref[...], sub, p_ref[...], g_ref, b_ref, wpg_ref, wpe_ref, alpha)
    xo_ref[...] = out
    xb_ref[...] = out.astype(BF16)


def _tok_spec(tm, width, col=0):
    return pl.BlockSpec((tm, width), lambda i: (i, col))


def _full_spec(shape):
    return pl.BlockSpec(shape, lambda i: (0,) * len(shape))


def _post_outputs(t, tm):
    return dict(
        out_specs=[_tok_spec(tm, D_MODEL), _tok_spec(tm, D_MODEL)],
        out_shape=[jax.ShapeDtypeStruct((t, D_MODEL), F32), jax.ShapeDtypeStruct((t, D_MODEL), BF16)],
        compiler_params=pltpu.CompilerParams(dimension_semantics=("parallel",), vmem_limit_bytes=VMEM_LIMIT),
    )


def _glu_post(yg, z, x, p, layer, ln_g, ln_b, w_glu, w_out, w_pg, w_pe, alpha):
    t = x.shape[0]
    tm = _token_tile(t, 512)
    ple = p.shape[2]
    n_grp, _, cw = yg.shape
    chunk = cw // SSM_GROUP
    return pl.pallas_call(
        functools.partial(_glu_post_kernel, alpha=alpha, chunk=chunk), grid=(t // tm,),
        in_specs=[pl.BlockSpec((n_grp, tm // chunk, cw), lambda i: (0, i, 0)), _tok_spec(tm, D_MODEL),
                  _tok_spec(tm, D_MODEL), pl.BlockSpec((None, tm, ple), lambda i: (layer, i, 0)),
                  _full_spec((1, D_MODEL)), _full_spec((1, D_MODEL)), _full_spec((D_MODEL, D_MODEL)),
                  _full_spec((D_MODEL, D_MODEL)), _full_spec((D_MODEL, D_MODEL)), _full_spec((ple, D_MODEL))],
        scratch_shapes=[pltpu.VMEM((D_MODEL // LANES, tm, LANES), F32),
                        pltpu.VMEM((DEINT_STAGE, tm // DEINT_STAGE, LANES), F32)],
        name="glu_post", **_post_outputs(t, tm),
    )(yg, z, x, p, ln_g, ln_b, w_glu, w_out, w_pg, w_pe)


def _comb_post(os_, lses, dils, z, z_col, x, p, layer, ln_g, ln_b, w_out, w_pg, w_pe, alpha):
    t = x.shape[0]
    tm = _token_tile(t, 512)
    ple = p.shape[2]
    if len(dils) == 1:
        group_specs = [_tok_spec(tm, D_MODEL)]
        scratch = []
    else:
        group_specs = [pl.BlockSpec((d, tm // d, D_MODEL), lambda i: (0, i, 0)) for d in dils]
        group_specs += [pl.BlockSpec((d, tm // d, LANES), lambda i: (0, i, 0)) for d in dils]
        scratch = [pltpu.VMEM((1, tm, LANES), F32) for d in dils if d > 1]
        scratch += [pltpu.VMEM((D_MODEL // LANES, tm, LANES), F32) for d in dils if d > 1]
    return pl.pallas_call(
        functools.partial(_comb_post_kernel, dils=tuple(dils), alpha=alpha), grid=(t // tm,),
        in_specs=group_specs
        + [_tok_spec(tm, D_MODEL, z_col), _tok_spec(tm, D_MODEL), pl.BlockSpec((None, tm, ple), lambda i: (layer, i, 0)),
           _full_spec((1, D_MODEL)), _full_spec((1, D_MODEL)), _full_spec((D_MODEL, D_MODEL)),
           _full_spec((D_MODEL, D_MODEL)), _full_spec((ple, D_MODEL))],
        scratch_shapes=scratch, name="attn_out_post", **_post_outputs(t, tm),
    )(*os_, *lses, z, x, p, ln_g, ln_b, w_out, w_pg, w_pe)


def _attn_prompt_kernel(q_ref, kp_ref, kc_ref, vp_ref, vc_ref, o_ref, lse_ref):
    ib = pl.program_id(2)
    tq = q_ref.shape[0]
    row = lax.broadcasted_iota(jnp.int32, (tq, 2 * tq), 0)
    col = lax.broadcasted_iota(jnp.int32, (tq, 2 * tq), 1)
    back = row + tq - col
    valid = (back >= 0) & (back <= WINDOW_KEYS) & ((col >= tq) | (ib > 0))
    lane = lax.broadcasted_iota(jnp.int32, (tq, LANES), 1)
    low = lane < HEAD_DIM
    head_of_lane = lane >> ((LANES // N_HEADS).bit_length() - 1)
    lse_tile = jnp.zeros((tq, LANES), F32)
    nt = (((1,), (1,)), ((), ()))
    for j in range(ATTN_WIDTH // LANES):
        sl = slice(j * LANES, (j + 1) * LANES)
        q2 = q_ref[:, sl]
        k2 = jnp.concatenate([kp_ref[:, sl], kc_ref[:, sl]], axis=0)
        v2 = jnp.concatenate([vp_ref[:, sl], vc_ref[:, sl]], axis=0)
        outs = []
        for sel in (low, jnp.logical_not(low)):
            qm = jnp.where(sel, q2, jnp.zeros_like(q2))
            s = lax.dot_general(qm, k2, nt, preferred_element_type=F32)
            s = jnp.where(valid, s, NEG_INF)
            m = jnp.max(s, axis=1, keepdims=True)
            pr = jnp.exp2(s - m)
            l = jnp.sum(pr, axis=1, keepdims=True)
            o = jnp.dot(pr.astype(BF16), v2, preferred_element_type=F32)
            outs.append((o / l, m + jnp.log2(l)))
        o_ref[:, sl] = jnp.where(low, outs[0][0], outs[1][0]).astype(o_ref.dtype)
        for hh in range(2):
            lse_tile = jnp.where(head_of_lane == 2 * j + hh, outs[hh][1], lse_tile)
    lse_ref[...] = lse_tile


def _attn_prompt_group(q, k, v, n_seq, seq_len):
    dil, rows, _ = q.shape
    tq = WINDOW_KEYS
    nb = seq_len // dil // tq
    assert rows == n_seq * nb * tq

    def cur(b, r, ib):
        return (r, b * nb + ib, 0)

    def prev(b, r, ib):
        return (r, b * nb + jnp.maximum(ib - 1, 0), 0)

    blk = (None, tq, D_MODEL)
    return pl.pallas_call(
        _attn_prompt_kernel, grid=(n_seq, dil, nb),
        in_specs=[pl.BlockSpec(blk, cur), pl.BlockSpec(blk, prev), pl.BlockSpec(blk, cur),
                  pl.BlockSpec(blk, prev), pl.BlockSpec(blk, cur)],
        out_specs=[pl.BlockSpec(blk, cur), pl.BlockSpec((None, tq, LANES), cur)],
        out_shape=[jax.ShapeDtypeStruct((dil, rows, D_MODEL), BF16), jax.ShapeDtypeStruct((dil, rows, LANES), F32)],
        compiler_params=pltpu.CompilerParams(dimension_semantics=("parallel", "parallel", "arbitrary"),
                                             vmem_limit_bytes=VMEM_LIMIT),
        name=f"attn_prompt_d{dil}",
    )(q, k, k, v, v)


SAMPLE_KEY_CHUNK = 512


def _attn_sample_kernel(q_ref, kvn_ref, c0_ref, c1_ref, c2_ref, o_ref, s_scr, m_scr):
    step = pl.program_id(1)
    ds = q_ref.shape[1]
    n_rows = N_HEADS * ds
    assert ds & (ds - 1) == 0 and n_rows == LANES
    rho = lax.broadcasted_iota(jnp.int32, (n_rows, ATTN_WIDTH), 0)
    lane = lax.broadcasted_iota(jnp.int32, (n_rows, ATTN_WIDTH), 1)
    head_mask = (rho >> (ds.bit_length() - 1)) == (lane >> (HEAD_DIM.bit_length() - 1))
    cache_refs = (c0_ref, c1_ref, c2_ref)
    nt = (((1,), (1,)), ((), ()))
    kvn = kvn_ref[0].astype(F32)

    def load_chunk(g, c0, cw):
        return cache_refs[g][0, 0, :, c0:c0 + cw].astype(BF16)

    def cache_valid(g, c0, cw):
        qi = lax.broadcasted_iota(jnp.int32, (n_rows, cw), 0) & (ds - 1)
        back = lax.broadcasted_iota(jnp.int32, (n_rows, cw), 1) + c0 - qi
        return (back >= 0) & ((back & (DILATIONS[g] - 1)) == 0)

    def new_block(g, off):
        piece = kvn[:, off + g * ATTN_WIDTH: off + (g + 1) * ATTN_WIDTH]
        pad = jnp.zeros((LANES - ds, ATTN_WIDTH), F32)
        return jnp.concatenate([piece, pad], axis=0).astype(BF16)

    chunks = []
    col = 0
    for g, c_ref in enumerate(cache_refs):
        width = c_ref.shape[3]
        for c0 in range(0, width, SAMPLE_KEY_CHUNK):
            cw = min(SAMPLE_KEY_CHUNK, width - c0)
            chunks.append((g, c0, cw, col))
            col += cw
    new_cols = [col + g * LANES for g in range(N_ATT_GROUPS)]

    @pl.when(step == 0)
    def _():
        q_all = q_ref[0].astype(F32)
        q_rows = []
        for g in range(N_ATT_GROUPS):
            qg = q_all[:, g * ATTN_WIDTH:(g + 1) * ATTN_WIDTH]
            q_rows.append(jnp.where(head_mask, jnp.concatenate([qg] * N_HEADS, axis=0), 0.0).astype(BF16))
        m = jnp.full((n_rows, 1), NEG_INF, F32)
        for g, c0, cw, col0 in chunks:
            s = jnp.dot(q_rows[g], load_chunk(g, c0, cw), preferred_element_type=F32)
            s = jnp.where(cache_valid(g, c0, cw), s, NEG_INF)
            s_scr[:, col0:col0 + cw] = s
            m = jnp.maximum(m, jnp.max(s, axis=1, keepdims=True))
        qi = lax.broadcasted_iota(jnp.int32, (n_rows, LANES), 0) & (ds - 1)
        kappa = lax.broadcasted_iota(jnp.int32, (n_rows, LANES), 1)
        for g in range(N_ATT_GROUPS):
            s = lax.dot_general(q_rows[g], new_block(g, 0), nt, preferred_element_type=F32)
            ahead = qi - kappa
            s = jnp.where((ahead >= 0) & ((ahead & (DILATIONS[g] - 1)) == 0) & (kappa < ds), s, NEG_INF)
            s_scr[:, new_cols[g]:new_cols[g] + LANES] = s
            m = jnp.maximum(m, jnp.max(s, axis=1, keepdims=True))
        m_scr[...] = jnp.broadcast_to(m, m_scr.shape)

    @pl.when(step == 1)
    def _():
        m = m_scr[:, 0:1]
        l = jnp.zeros((n_rows, 1), F32)
        acc = jnp.zeros((n_rows, ATTN_WIDTH), F32)
        for g, c0, cw, col0 in chunks:
            pr = jnp.exp2(s_scr[:, col0:col0 + cw] - m)
            l = l + jnp.sum(pr, axis=1, keepdims=True)
            acc = acc + lax.dot_general(pr.astype(BF16), load_chunk(g, c0, cw), nt, preferred_element_type=F32)
        for g in range(N_ATT_GROUPS):
            pr = jnp.exp2(s_scr[:, new_cols[g]:new_cols[g] + LANES] - m)
            l = l + jnp.sum(pr, axis=1, keepdims=True)
            acc = acc + jnp.dot(pr.astype(BF16), new_block(g, N_ATT_GROUPS * ATTN_WIDTH),
                                preferred_element_type=F32)
        acc = acc / l
        acc3 = acc.reshape(N_HEADS, ds, ATTN_WIDTH)
        low = lax.broadcasted_iota(jnp.int32, (ds, LANES), 1) < HEAD_DIM
        for c in range(ATTN_WIDTH // LANES):
            sl = slice(c * LANES, (c + 1) * LANES)
            o_ref[0, :, sl] = jnp.where(low, acc3[2 * c][:, sl], acc3[2 * c + 1][:, sl])


def _cache_views(caches):
    views = []
    for cache, dil in zip(caches, DILATIONS):
        n, width = cache.shape[:2]
        assert width == WINDOW_KEYS * dil and cache.shape[2:] == (2, N_HEADS, HEAD_DIM)
        views.append(cache.transpose(0, 2, 3, 4, 1).reshape(n, 2, ATTN_WIDTH, width))
    return views


def _attn_sample(qz_s, kv_s, views, n_seq, ds):
    q3 = qz_s.reshape(n_seq, ds, qz_s.shape[1])
    kv3 = kv_s.reshape(n_seq, ds, kv_s.shape[1])
    n_cols = sum(v.shape[3] for v in views) + N_ATT_GROUPS * LANES

    def tok_spec(width):
        return pl.BlockSpec((1, ds, width), lambda n, s: (n, 0, 0))

    def cache_spec(width):
        return pl.BlockSpec((1, 1, ATTN_WIDTH, width), lambda n, s: (n, s, 0, 0))

    o = pl.pallas_call(
        _attn_sample_kernel, grid=(n_seq, 2),
        in_specs=[tok_spec(q3.shape[2]), tok_spec(kv3.shape[2])] + [cache_spec(v.shape[3]) for v in views],
        out_specs=tok_spec(ATTN_WIDTH), out_shape=jax.ShapeDtypeStruct((n_seq, ds, ATTN_WIDTH), F32),
        scratch_shapes=[pltpu.VMEM((N_HEADS * ds, n_cols), F32), pltpu.VMEM((N_HEADS * ds, LANES), F32)],
        compiler_params=pltpu.CompilerParams(dimension_semantics=("parallel", "arbitrary"),
                                             vmem_limit_bytes=VMEM_LIMIT),
        name="attn_sample",
    )(q3, kv3, *views)
    return o.reshape(n_seq * ds, ATTN_WIDTH)


def kernel(x_prompt, x_sample, state_ssm_re, state_ssm_im, cache_kv_w128, cache_kv_w512, cache_kv_w2048, p_prompt, p_sample, ln_g, ln_b, w_pe, w_pg, w_in_a, a_re, a_im, log_dt, b_re, b_im, c_re, c_im, d_skip, w_glu, w_out_a, w_kv, w_in_b, w_out_b):
    depth = ln_g.shape[0]
    n_a = w_in_a.shape[0]
    batch, seq, _ = x_prompt.shape
    n_s, ds, _ = x_sample.shape
    past = cache_kv_w2048.shape[1]
    assert ds * 2 == SSM_CHUNK and ds <= DILATIONS[2] and past == WINDOWS[2] and seq % (DILATIONS[2] * WINDOW_KEYS) == 0
    alpha = (2 * depth) ** 0.25
    t_p, t_s = batch * seq, n_s * ds
    caches =(cache_kv_w128, cache_kv_w512, cache_kv_w2048)

    xs = [x_prompt.reshape(t_p, D_MODEL), x_sample.reshape(t_s, D_MODEL)]
    xbs = list(xs)
    ps = [p_prompt.reshape(depth, t_p, -1), p_sample.reshape(depth, t_s, -1)]
    wb = lambda w: w.astype(BF16)
    h_prompt, h_sample = [], []

    for i in range(n_a):
        w_in = wb(w_in_a[i])
        (xg, z_p), (xsg, z_s) = _project_ssm(xbs[0], w_in, SSM_CHUNK), _project_ssm(xbs[1], w_in, ds)
        ops = _ssm_operators(a_re[i], a_im[i], log_dt[i], b_re[i], b_im[i], c_re[i], c_im[i], d_skip[i])
        h0g = jnp.concatenate([state_ssm_re[i], state_ssm_im[i]], axis=-1).astype(F32).transpose(1, 0, 2)
        yg, hl, ysg, hs = _ssm(xg, xsg, h0g, ops, n_seq=batch)
        h_prompt.append(hl.transpose(1, 0, 2))
        h_sample.append(hs.transpose(1, 0, 2))
        for k, (y, z) in enumerate(((yg, z_p), (ysg, z_s))):
            xs[k], xbs[k] = _glu_post(y, z, xs[k], ps[k], i, ln_g[i:i + 1], ln_b[i:i + 1], wb(w_glu[i]),
                                      wb(w_out_a[i]), wb(w_pg[i]), wb(w_pe[i]), alpha)

    pos = [jnp.arange(seq, dtype=F32), jnp.tile(past + jnp.arange(ds, dtype=F32), n_s)]
    rope_k = [_rope_tables(p_, 1.0) for p_ in pos]
    q_scale = HEAD_DIM ** -0.5 * math.log2(math.e)
    rope_q = [tuple(q_scale * tab for tab in tabs) for tabs in rope_k]
    q_width = N_ATT_GROUPS * ATTN_WIDTH
    n_grp = N_ATT_GROUPS
    w_kv_b = wb(w_kv)
    k_p = [_project_deint(xbs[0], w_kv_b, grp, DILATIONS[grp], rope_k[0]) for grp in range(n_grp)]
    v_p = [_project_deint(xbs[0], w_kv_b, n_grp + grp, DILATIONS[grp]) for grp in range(n_grp)]
    kv_s = _project(xbs[1], w_kv_b, rope=rope_k[1], rot_cols=q_width, tn=512)
    views = _cache_views(caches)

    for i in range(n_a, depth):
        j = i - n_a
        w_in = wb(w_in_b[j])
        post_w = (ln_g[i:i + 1], ln_b[i:i + 1], wb(w_out_b[j]), wb(w_pg[i]), wb(w_pe[i]), alpha)
        q_p = [_project_deint(xbs[0], w_in, grp, DILATIONS[grp], rope_q[0]) for grp in range(n_grp)]
        z_p = _project_deint(xbs[0], w_in, n_grp, 1).reshape(t_p, D_MODEL)
        parts = [_attn_prompt_group(q_p[grp], k_p[grp], v_p[grp], batch, seq) for grp in range(n_grp)]
        qz_s = _project(xbs[1], w_in, rope=rope_q[1], rot_cols=q_width, tn=512)
        o_s = _attn_sample(qz_s, kv_s, views, n_s, ds)
        xs[0], xbs[0] = _comb_post([pt[0] for pt in parts], [pt[1] for pt in parts], DILATIONS, z_p, 0,
                                   xs[0], ps[0], i, *post_w)
        xs[1], xbs[1] = _comb_post([o_s], [], (1,), qz_s, n_grp, xs[1], ps[1], i, *post_w)

    y_prompt = xs[0].reshape(batch, seq, D_MODEL)
    y_sample = xs[1].reshape(n_s, ds, D_MODEL)
    hp = jnp.stack(h_prompt)
    hsm = jnp.stack(h_sample)

    def prompt_window(arr, dil, width):
        per = arr.reshape(dil, batch, seq // dil, ATTN_WIDTH)[:, :, (seq - width) // dil:]
        return per.transpose(1, 2, 0, 3).reshape(batch, width, N_HEADS, HEAD_DIM).astype(F32)

    kv_prompt = [jnp.stack([prompt_window(k_p[grp], DILATIONS[grp], min(w, seq)),
                            prompt_window(v_p[grp], DILATIONS[grp], min(w, seq))], axis=2)
                 for grp, w in enumerate(WINDOWS)]
    kv_s6 = kv_s.astype(F32).reshape(n_s, ds, 2, N_ATT_GROUPS, N_HEADS, HEAD_DIM)
    kv_sample = [kv_s6[:, :, :, grp] for grp in range(N_ATT_GROUPS)]
    return (y_prompt, y_sample, hp[..., :SSM_STATE], hp[..., SSM_STATE:], hsm[..., :SSM_STATE], hsm[..., SSM_STATE:],
            *kv_prompt, *kv_sample)
```

```python
import functools
import math

import jax
import jax.numpy as jnp
from jax import lax
from jax.experimental import pallas as pl
from jax.experimental.pallas import tpu as pltpu

F32 = jnp.float32
BF16 = jnp.bfloat16

D_MODEL = 1024
N_SSM_GROUPS = 64
SSM_GROUP = 16
SSM_STATE = 64
SSM_CHUNK = 16
N_ATT_GROUPS = 3
WINDOWS = (128, 512, 2048)
DILATIONS = (1, 4, 16)
HEAD_DIM = 64
N_HEADS = 16
ATTN_WIDTH = N_HEADS * HEAD_DIM
ROT_DIM = HEAD_DIM // 4
ROPE_THETA = 500000.0
WINDOW_KEYS = 128
LN_EPS = 1e-5
NEG_INF = -1e30
LANES = 128
SUBLANES = 8
VMEM_LIMIT = 52 * 1024 * 1024


def _sigmoid(x):
    return 0.5 + 0.5 * jnp.tanh(0.5 * x)


def _gelu_tanh(x):
    return 0.5 * x * (1.0 + jnp.tanh(math.sqrt(2.0 / math.pi) * (x + 0.044715 * (x * x * x))))


def _mm_kernel(x_ref, w_ref, o_ref):
    o_ref[...] = jnp.dot(x_ref[...], w_ref[...], preferred_element_type=F32).astype(o_ref.dtype)


def _mm_rope_kernel(x_ref, w_ref, cos_ref, sa_ref, sb_ref, o_ref, *, rot_blocks):
    j = pl.program_id(1)
    acc = jnp.dot(x_ref[...], w_ref[...], preferred_element_type=F32)

    @pl.when(j < rot_blocks)
    def _():
        o_ref[...] = _rope(acc, cos_ref[...], sa_ref[...], sb_ref[...]).astype(o_ref.dtype)

    @pl.when(j >= rot_blocks)
    def _():
        o_ref[...] = acc.astype(o_ref.dtype)


def _token_tile(n_tokens, want):
    tm = min(want, n_tokens)
    assert n_tokens % tm == 0
    return tm


def _project(x, w, *, rope=None, rot_cols=0, tn=512, out_dtype=BF16):
    t, k = x.shape
    n = w.shape[1]
    tm = _token_tile(t, 1024)
    grid = (t // tm, n // tn)
    x_spec = pl.BlockSpec((tm, k), lambda i, j: (i, 0))
    w_spec = pl.BlockSpec((k, tn), lambda i, j: (0, j))
    o_spec = pl.BlockSpec((tm, tn), lambda i, j: (i, j))
    params = pltpu.CompilerParams(dimension_semantics=("parallel", "arbitrary"), vmem_limit_bytes=VMEM_LIMIT)
    if rope is None:
        return pl.pallas_call(
            _mm_kernel, grid=grid, in_specs=[x_spec, w_spec], out_specs=o_spec,
            out_shape=jax.ShapeDtypeStruct((t, n), out_dtype), compiler_params=params, name="proj",
        )(x, w)
    assert rot_cols % tn == 0
    tab_spec = pl.BlockSpec((tm, LANES), lambda i, j: (i, 0))
    return pl.pallas_call(
        functools.partial(_mm_rope_kernel, rot_blocks=rot_cols // tn), grid=grid,
        in_specs=[x_spec, w_spec, tab_spec, tab_spec, tab_spec], out_specs=o_spec,
        out_shape=jax.ShapeDtypeStruct((t, n), out_dtype), compiler_params=params, name="proj_rope",
    )(x, w, *rope)


PROJ_SUB_ROWS = 512
DEINT_STAGE = 4


def _rope(acc, cos, sa, sb):
    half = ROT_DIM // 2
    outs = []
    for c in range(acc.shape[1] // LANES):
        a = acc[:, c * LANES:(c + 1) * LANES]
        outs.append(a * cos + pltpu.roll(a, LANES - half, 1) * sa + pltpu.roll(a, half, 1) * sb)
    return jnp.concatenate(outs, axis=1)


def _mm_deint_kernel(*refs, dil, rope):
    if rope:
        x_ref, w_ref, cos_ref, sa_ref, sb_ref, o_ref = refs[:6]
    else:
        x_ref, w_ref, o_ref = refs[:3]
    tm = x_ref.shape[0]
    sub = min(tm, PROJ_SUB_ROWS)
    for b in range(tm // sub):
        rs = slice(b * sub, (b + 1) * sub)
        acc = jnp.dot(x_ref[rs, :], w_ref[...], preferred_element_type=F32)
        if rope:
            acc = _rope(acc, cos_ref[rs, :], sa_ref[rs, :], sb_ref[rs, :])
        if dil == 1:
            o_ref[0, rs, :] = acc.astype(o_ref.dtype)
            continue
        acc_scr, mid_scr = refs[-2:]
        rows = sub // dil
        f1 = min(dil, DEINT_STAGE)
        f2 = dil // f1
        for c in range(acc.shape[1] // LANES):
            sl = slice(c * LANES, (c + 1) * LANES)
            acc_scr[c, rs, :] = acc[:, sl]
            for r1 in range(f1):
                if f2 == 1:
                    o_ref[r1, b * rows:(b + 1) * rows, sl] = (
                        acc_scr[c, pl.ds(b * sub + r1, rows, stride=dil), :].astype(o_ref.dtype))
                    continue
                mid_scr[r1] = acc_scr[c, pl.ds(b * sub + r1, sub // f1, stride=f1), :]
                for r2 in range(f2):
                    o_ref[r1 + f1 * r2, b * rows:(b + 1) * rows, sl] = (
                        mid_scr[r1, pl.ds(r2, rows, stride=f2), :].astype(o_ref.dtype))


def _project_deint(x, w, col, dil, rope=None):
    t, k = x.shape
    tm = _token_tile(t, 1024)
    wn = D_MODEL
    in_specs = [pl.BlockSpec((tm, k), lambda i: (i, 0)), pl.BlockSpec((k, wn), lambda i: (0, col))]
    args = [x, w]
    if rope is not None:
        tab_blocks = rope[0].shape[0] // tm
        in_specs += [pl.BlockSpec((tm, LANES), lambda i: (i % tab_blocks, 0))] * 3
        args += list(rope)
    return pl.pallas_call(
        functools.partial(_mm_deint_kernel, dil=dil, rope=rope is not None), grid=(t // tm,),
        in_specs=in_specs, out_specs=pl.BlockSpec((dil, tm // dil, wn), lambda i: (0, i, 0)),
        out_shape=jax.ShapeDtypeStruct((dil, t // dil, wn), BF16),
        scratch_shapes=[pltpu.VMEM((wn // LANES, tm, LANES), F32),
                        pltpu.VMEM((DEINT_STAGE, min(tm, PROJ_SUB_ROWS) // DEINT_STAGE, LANES), F32)] if dil > 1 else [],
        compiler_params=pltpu.CompilerParams(dimension_semantics=("parallel",), vmem_limit_bytes=VMEM_LIMIT),
        name=f"proj_deint{dil}",
    )(*args)


GROUPS_PER_SLAB = LANES // SSM_GROUP


def _slot_position(lanes_shape):
    return lax.broadcasted_iota(jnp.int32, lanes_shape, 1) >> (SSM_GROUP.bit_length() - 1)


def _mm_ssm_kernel(x_ref, w_ref, xg_ref, z_ref, slab_scr, mid_scr, *, chunk):
    tm = x_ref.shape[0]
    n_chunks = tm // chunk
    x = x_ref[...].astype(BF16)
    z_ref[...] = jnp.dot(x, w_ref[:, D_MODEL:], preferred_element_type=F32).astype(z_ref.dtype)
    u = jnp.dot(x, w_ref[:, :D_MODEL], preferred_element_type=F32)
    slot = _slot_position((n_chunks, LANES))
    gps = GROUPS_PER_SLAB
    f1 = DEINT_STAGE
    f2 = chunk // f1
    for c in range(D_MODEL // LANES):
        slab_scr[c] = u[:, c * LANES:(c + 1) * LANES]
        for q in range(f1):
            mid_scr[q] = slab_scr[c, pl.ds(q, tm // f1, stride=f1), :]
        for j in range(chunk // gps):
            rot = []
            for r in range(gps):
                s = gps * j + r
                rows = mid_scr[s % f1, pl.ds(s // f1, n_chunks, stride=f2), :]
                rot.append(rows if r == 0 else pltpu.roll(rows, r * SSM_GROUP, 1))
            for gg in range(gps):
                tile = rot[0]
                for r in range(1, gps):
                    tile = jnp.where(slot == (gg + r) % gps, rot[r], tile)
                xg_ref[c * gps + gg, :, j * LANES:(j + 1) * LANES] = tile.astype(xg_ref.dtype)


def _project_ssm(x, w, chunk):
    t, k = x.shape
    tm = _token_tile(t, 512)
    cw = chunk * SSM_GROUP
    return pl.pallas_call(
        functools.partial(_mm_ssm_kernel, chunk=chunk), grid=(t // tm,),
        in_specs=[pl.BlockSpec((tm, k), lambda i: (i, 0)), pl.BlockSpec(w.shape, lambda i: (0, 0))],
        out_specs=[pl.BlockSpec((N_SSM_GROUPS, tm // chunk, cw), lambda i: (0, i, 0)),
                   pl.BlockSpec((tm, D_MODEL), lambda i: (i, 0))],
        out_shape=[jax.ShapeDtypeStruct((N_SSM_GROUPS, t // chunk, cw), BF16),
                   jax.ShapeDtypeStruct((t, D_MODEL), BF16)],
        scratch_shapes=[pltpu.VMEM((D_MODEL // LANES, tm, LANES), F32),
                        pltpu.VMEM((DEINT_STAGE, tm // DEINT_STAGE, LANES), F32)],
        compiler_params=pltpu.CompilerParams(dimension_semantics=("parallel",), vmem_limit_bytes=VMEM_LIMIT),
        name=f"proj_ssm{chunk}",
    )(x, w)


def _rope_tables(pos, scale):
    half = ROT_DIM // 2
    inv_freq = ROPE_THETA ** (-jnp.arange(0, ROT_DIM, 2, dtype=F32) / ROT_DIM)
    ang = pos[:, None] * inv_freq[None, :]
    cos, sin = jnp.cos(ang), jnp.sin(ang)
    t = pos.shape[0]
    ones = jnp.ones((t, HEAD_DIM - ROT_DIM), F32)
    zeros = jnp.zeros((t, HEAD_DIM - ROT_DIM), F32)
    zhalf = jnp.zeros((t, half), F32)
    cos64 = jnp.concatenate([cos, cos, ones], axis=1)
    sa64 = jnp.concatenate([-sin, zhalf, zeros], axis=1)
    sb64 = jnp.concatenate([zhalf, sin, zeros], axis=1)
    return tuple(scale * jnp.concatenate([a, a], axis=1) for a in (cos64, sa64, sb64))


COEF_BLOCK_STEPS = tuple(SSM_CHUNK << k for k in range(3))
COEF_CARRY_STEPS = tuple(SSM_CHUNK * SUBLANES << k for k in range(6))
COEF_HALF_ROW = len(COEF_BLOCK_STEPS) + len(COEF_CARRY_STEPS)
COEF_IN_BLOCK_ROW = 16
N_COEF_ROWS = COEF_IN_BLOCK_ROW + SUBLANES


def _cpair_mul(a, b):
    return a[0] * b[0] - a[1] * b[1], a[0] * b[1] + a[1] * b[0]


def _abar_and_factor(a_re, a_im, dt):
    mag = jnp.exp(dt * a_re)
    ab = (mag * jnp.cos(dt * a_im), mag * jnp.sin(dt * a_im))
    den = a_re * a_re + a_im * a_im
    nr, ni = ab[0] - 1.0, ab[1]
    fac = ((nr * a_re + ni * a_im) / den, (ni * a_re - nr * a_im) / den)
    return ab, fac


def _powers(ab, n):
    out = [(jnp.ones_like(ab[0]), jnp.zeros_like(ab[0]))]
    for _ in range(n):
        out.append(_cpair_mul(out[-1], ab))
    return out


OPS_GROUPS_PER_STEP = 1


def _ssm_ops_kernel(*refs):
    io_refs, (rhs_scr, t_scr) = refs[:-2], refs[-2:]
    for j in range(OPS_GROUPS_PER_STEP):
        group = pl.program_id(0) * OPS_GROUPS_PER_STEP + j
        _ssm_ops_group(group, *[r.at[pl.ds(j, 1)] for r in io_refs], rhs_scr.at[j], t_scr.at[j])


def _roll_lane_tiles(x, shift):
    tiles = [pltpu.roll(x[:, k * LANES:(k + 1) * LANES], shift, 1) for k in range(x.shape[1] // LANES)]
    return tiles[0] if len(tiles) == 1 else jnp.concatenate(tiles, axis=1)


def _ssm_ops_group(group, arow_ref, bt_ref, c_ref, d_ref, t_ref, bin_ref, cout_ref, ca_ref, cb_ref,
                   rhs_scr, t_scr):
    c, m, p = SSM_CHUNK, SSM_GROUP, SSM_STATE
    gps = GROUPS_PER_SLAB
    hi = lax.Precision.HIGHEST
    rot = group & (gps - 1)

    def slot_row(s):
        return pl.multiple_of((s // gps * gps + ((rot + s % gps) & (gps - 1))) * m, m)

    ab, fac = _abar_and_factor(arow_ref[0, 0:1, :], arow_ref[0, 1:2, :], arow_ref[0, 2:3, :])
    bbar = _cpair_mul((bt_ref[0, 0], bt_ref[0, 1]), fac)
    pw = _powers(ab, c)
    for s in range(c):
        blk = _cpair_mul(bbar, pw[c - 1 - s])
        bin_ref[0, pl.ds(slot_row(s), m), 0:p] = blk[0].astype(bin_ref.dtype)
        bin_ref[0, pl.ds(slot_row(s), m), p:2 * p] = blk[1].astype(bin_ref.dtype)
    coef = {}
    cur = pw[c]
    for step in COEF_BLOCK_STEPS + COEF_CARRY_STEPS:
        coef[step] = cur
        cur = _cpair_mul(cur, cur)
    rows = [coef[s] for s in COEF_BLOCK_STEPS + COEF_CARRY_STEPS] + [pw[c // 2]]
    rows += [pw[0]] * (COEF_IN_BLOCK_ROW - len(rows))
    in_block = [pw[0]]
    for _ in range(SUBLANES - 1):
        in_block.append(_cpair_mul(in_block[-1], pw[c]))
    rows += in_block
    for k, (re, im) in enumerate(rows):
        ca_ref[0, k:k + 1, 0:p] = re
        ca_ref[0, k:k + 1, p:2 * p] = re
        cb_ref[0, k:k + 1, 0:p] = -im
        cb_ref[0, k:k + 1, p:2 * p] = im

    cm = (c_ref[0, 0], c_ref[0, 1])
    for t in range(c + 1):
        blk = _cpair_mul(cm, pw[t])
        if t < c:
            rhs_scr[0, t * m:(t + 1) * m, :] = blk[0]
            rhs_scr[1, t * m:(t + 1) * m, :] = blk[1]
        if t >= 1:
            cout_ref[0, pl.ds(slot_row(t - 1), m), 0:p] = blk[0].astype(cout_ref.dtype)
            cout_ref[0, pl.ds(slot_row(t - 1), m), p:2 * p] = (-blk[1]).astype(cout_ref.dtype)
    lane_shift = rot * m

    nt = (((1,), (1,)), ((), ()))
    kern = (lax.dot_general(bbar[0], rhs_scr[0], nt, precision=hi, preferred_element_type=F32)
            - lax.dot_general(bbar[1], rhs_scr[1], nt, precision=hi, preferred_element_type=F32))
    lane = lax.broadcasted_iota(jnp.int32, kern.shape, 1)
    row = lax.broadcasted_iota(jnp.int32, kern.shape, 0)
    for s in range(c):
        shifted = kern if s == 0 else jnp.where(lane >= s * m, pltpu.roll(kern, s * m, 1), 0.0)
        shifted = shifted + jnp.where(lane == row + s * m, d_ref[0], 0.0)
        t_scr[pl.ds(slot_row(s), m), :] = shifted
    t_ref[0] = _roll_lane_tiles(t_scr[...], lane_shift).astype(t_ref.dtype)


def _ssm_operators(a_re, a_im, log_dt, b_re, b_im, c_re, c_im, d_skip):
    g, p, m, c = N_SSM_GROUPS, SSM_STATE, SSM_GROUP, SSM_CHUNK
    dt = jnp.broadcast_to(jnp.exp(log_dt.astype(F32))[:, None], (g, p))
    arow = jnp.stack([a_re.astype(F32), a_im.astype(F32), dt], axis=1)
    bt = jnp.stack([b_re, b_im], axis=1).astype(F32).transpose(0, 1, 3, 2)
    cm = jnp.stack([c_re, c_im], axis=1).astype(F32)
    d_rep = jnp.tile(d_skip.astype(F32).reshape(g, 1, m), (1, 1, c))
    cw = c * m

    gps = OPS_GROUPS_PER_STEP

    def spec(*shape):
        return pl.BlockSpec((gps,) + shape, lambda i: (i,) + (0,) * len(shape))

    return pl.pallas_call(
        _ssm_ops_kernel, grid=(g // gps,),
        in_specs=[spec(3, p), spec(2, m, p), spec(2, m, p), spec(1, cw)],
        out_specs=[spec(cw, cw), spec(cw, 2 * p), spec(cw, 2 * p), spec(N_COEF_ROWS, 2 * p), spec(N_COEF_ROWS, 2 * p)],
        out_shape=[jax.ShapeDtypeStruct((g, cw, cw), BF16), jax.ShapeDtypeStruct((g, cw, 2 * p), BF16),
                   jax.ShapeDtypeStruct((g, cw, 2 * p), BF16), jax.ShapeDtypeStruct((g, N_COEF_ROWS, 2 * p), F32),
                   jax.ShapeDtypeStruct((g, N_COEF_ROWS, 2 * p), F32)],
        scratch_shapes=[pltpu.VMEM((gps, 2, cw, p), F32), pltpu.VMEM((gps, cw, cw), F32)],
        compiler_params=pltpu.CompilerParams(dimension_semantics=("parallel",)),
        name="s5_operators",
    )(arow, bt, cm, d_rep)


def _swap_halves(x):
    return pltpu.roll(x, SSM_STATE, x.ndim - 1)


def _cmul(x, xs, ca, cb):
    return x * ca + xs * cb


def _ssm_kernel(x_ref, xs_ref, h0_ref, t_ref, bin_ref, cout_ref, ca_ref, cb_ref,
                y_ref, hl_ref, ys_ref, hs_ref, l_scr, e_scr, *, n_seq, rows_per_seq):
    rows = n_seq * rows_per_seq
    n_blocks = rows_per_seq // SUBLANES
    ca = ca_ref[0]
    cb = cb_ref[0]
    x = x_ref[0]
    v = jnp.dot(x, bin_ref[0], preferred_element_type=F32)
    rmod = lax.broadcasted_iota(jnp.int32, (rows, LANES), 0) & (SUBLANES - 1)

    for k, dist in enumerate((1, 2, 4)):
        sh = jnp.where(rmod >= dist, pltpu.roll(v, dist, 0), 0.0)
        v = v + _cmul(sh, _swap_halves(sh), ca[k:k + 1], cb[k:k + 1])
    l_scr[...] = v

    assert n_blocks & (n_blocks - 1) == 0 and n_blocks <= 1 << len(COEF_CARRY_STEPS)
    e = l_scr[pl.ds(SUBLANES - 1, rows // SUBLANES, stride=SUBLANES), :]
    bmod = lax.broadcasted_iota(jnp.int32, e.shape, 0) & (n_blocks - 1)
    for k in range(n_blocks.bit_length() - 1):
        row = len(COEF_BLOCK_STEPS) + k
        sh = jnp.where(bmod >= (1 << k), pltpu.roll(e, 1 << k, 0), 0.0)
        e = e + _cmul(sh, _swap_halves(sh), ca[row:row + 1], cb[row:row + 1])
    hl_ref[0] = jnp.concatenate([e[(s + 1) * n_blocks - 1:(s + 1) * n_blocks] for s in range(n_seq)], axis=0)
    e_prev = jnp.where(bmod >= 1, pltpu.roll(e, 1, 0), 0.0)
    for r in range(SUBLANES):
        e_scr[pl.ds(r, rows // SUBLANES, stride=SUBLANES), :] = e_prev

    lshift = jnp.where(rmod >= 1, pltpu.roll(v, 1, 0), 0.0)
    ebc = e_scr[...]
    pwa = ca[COEF_IN_BLOCK_ROW:COEF_IN_BLOCK_ROW + SUBLANES]
    pwb = cb[COEF_IN_BLOCK_ROW:COEF_IN_BLOCK_ROW + SUBLANES]
    shape3 = (rows // SUBLANES, SUBLANES, LANES)
    h = (lshift.reshape(shape3) + ebc.reshape(shape3) * pwa[None]
         + _swap_halves(ebc).reshape(shape3) * pwb[None]).reshape(rows, LANES)
    nt = (((1,), (1,)), ((), ()))
    y = jnp.dot(x, t_ref[0], preferred_element_type=F32)
    y = y + lax.dot_general(h.astype(BF16), cout_ref[0], nt, preferred_element_type=F32)
    y_ref[0] = y.astype(y_ref.dtype)

    half = SSM_CHUNK * SSM_GROUP // 2
    xs = xs_ref[0]
    h0 = h0_ref[0]
    ys = jnp.dot(xs, t_ref[0, :half, :half], preferred_element_type=F32)
    ys = ys + lax.dot_general(h0.astype(BF16), cout_ref[0, :half, :], nt, preferred_element_type=F32)
    ys_ref[0] = ys.astype(ys_ref.dtype)
    hs = _cmul(h0, _swap_halves(h0), ca[COEF_HALF_ROW:COEF_HALF_ROW + 1], cb[COEF_HALF_ROW:COEF_HALF_ROW + 1])
    hs_ref[0] = hs + jnp.dot(xs, bin_ref[0, half:, :], preferred_element_type=F32)


def _ssm(xg, xsg, h0g, ops, *, n_seq):
    t_op, b_in, c_out, ca, cb = ops
    g, rows, cw = xg.shape
    n_s = xsg.shape[1]
    hw = xsg.shape[2]

    def spec(shape):
        return pl.BlockSpec((1,) + shape, lambda i: (i, 0, 0))

    return pl.pallas_call(
        functools.partial(_ssm_kernel, n_seq=n_seq, rows_per_seq=rows // n_seq),
        grid=(g,),
        in_specs=[spec((rows, cw)), spec((n_s, hw)), spec((n_s, LANES)), spec((cw, cw)), spec((cw, LANES)),
                  spec((cw, LANES)), spec((N_COEF_ROWS, LANES)), spec((N_COEF_ROWS, LANES))],
        out_specs=[spec((rows, cw)), spec((n_seq, LANES)), spec((n_s, hw)), spec((n_s, LANES))],
        out_shape=[jax.ShapeDtypeStruct((g, rows, cw), BF16), jax.ShapeDtypeStruct((g, n_seq, LANES), F32),
                   jax.ShapeDtypeStruct((g, n_s, hw), BF16), jax.ShapeDtypeStruct((g, n_s, LANES), F32)],
        scratch_shapes=[pltpu.VMEM((rows, LANES), F32)] * 2,
        compiler_params=pltpu.CompilerParams(dimension_semantics=("parallel",), vmem_limit_bytes=VMEM_LIMIT),
        name="s5_scan",
    )(xg, xsg, h0g, t_op, b_in, c_out, ca, cb)


def _post_layer(x, sub, p, g_ref, b_ref, wpg_ref, wpe_ref, alpha):
    r = alpha * x + sub
    mu = jnp.mean(r, axis=-1, keepdims=True)
    cen = r - mu
    var = jnp.mean(cen * cen, axis=-1, keepdims=True)
    h = cen * lax.rsqrt(var + LN_EPS) * g_ref[...] + b_ref[...]
    gate = _sigmoid(jnp.dot(h.astype(BF16), wpg_ref[...], preferred_element_type=F32))
    ple = jnp.dot(p.astype(BF16), wpe_ref[...], preferred_element_type=F32)
    return h + gate * ple


def _glu_post_kernel(yg_ref, z_ref, x_ref, p_ref, g_ref, b_ref, wglu_ref, wout_ref, wpg_ref, wpe_ref,
                     xo_ref, xb_ref, slab_scr, mid_scr, *, alpha, chunk):
    n_chunks = yg_ref.shape[1]
    slot = _slot_position((n_chunks, LANES))
    gps = GROUPS_PER_SLAB
    f1 = DEINT_STAGE
    f2 = chunk // f1
    tm = n_chunks * chunk
    for c in range(D_MODEL // LANES):
        for j in range(chunk // gps):
            tiles = [yg_ref[c * gps + gg, :, j * LANES:(j + 1) * LANES].astype(F32) for gg in range(gps)]
            for r in range(gps):
                picked = tiles[0]
                for gg in range(1, gps):
                    picked = jnp.where(slot == (gg + r) % gps, tiles[gg], picked)
                rows = picked if r == 0 else pltpu.roll(picked, LANES - r * SSM_GROUP, 1)
                t = gps * j + r
                mid_scr[t % f1, pl.ds(t // f1, n_chunks, stride=f2), :] = rows
        for q in range(f1):
            slab_scr[c, pl.ds(q, tm // f1, stride=f1), :] = mid_scr[q]
    y = jnp.concatenate([slab_scr[c] for c in range(D_MODEL // LANES)], axis=1)
    g = _gelu_tanh(y)
    t = jnp.dot(g.astype(BF16), wglu_ref[...], preferred_element_type=F32)
    z = z_ref[...].astype(F32)
    glu = g * _sigmoid(t) * (z * _sigmoid(z))
    sub = jnp.dot(glu.astype(BF16), wout_ref[...], preferred_element_type=F32)
    out = _post_layer(x_ref[...], sub, p_ref[...], g_ref, b_ref, wpg_ref, wpe_ref, alpha)
    xo_ref[...] = out
    xb_ref[...] = out.astype(BF16)


def _interleave_rows(ref, scr, dil):
    if dil == 1:
        return ref[0].astype(F32)
    rows = ref.shape[1]
    slabs = []
    for c in range(ref.shape[2] // LANES):
        for r in range(dil):
            scr[c, pl.ds(r, rows, stride=dil), :] = ref[r, :, c * LANES:(c + 1) * LANES].astype(F32)
        slabs.append(scr[c])
    return slabs[0] if len(slabs) == 1 else jnp.concatenate(slabs, axis=1)


def _comb_post_kernel(*refs, dils, alpha):
    n = len(dils)
    if n == 1:
        o_ref = refs[0]
        z_ref, x_ref, p_ref, g_ref, b_ref, wout_ref, wpg_ref, wpe_ref, xo_ref, xb_ref = refs[1:]
        o = o_ref[...].astype(F32)
    else:
        o_refs, lse_refs = refs[:n], refs[n:2 * n]
        z_ref, x_ref, p_ref, g_ref, b_ref, wout_ref, wpg_ref, wpe_ref, xo_ref, xb_ref = refs[2 * n:2 * n + 10]
        scrs = list(refs[2 * n + 10:])
        lses = [_interleave_rows(r, scrs.pop(0) if d > 1 else None, d) for r, d in zip(lse_refs, dils)]
        top = functools.reduce(jnp.maximum, lses)
        ws = [jnp.exp2(l - top) for l in lses]
        tot = functools.reduce(lambda a, c: a + c, ws)
        src = lax.broadcasted_iota(jnp.int32, (LANES, ATTN_WIDTH), 0)
        dst = lax.broadcasted_iota(jnp.int32, (LANES, ATTN_WIDTH), 1)
        lanes_per_head = LANES // N_HEADS
        expand = jnp.where(src == (dst >> (HEAD_DIM.bit_length() - 1)) * lanes_per_head, 1.0, 0.0).astype(BF16)
        o = None
        for w, r, d in zip(ws, o_refs, dils):
            a = w / tot
            hi = a.astype(BF16)
            lo = (a - hi.astype(F32)).astype(BF16)
            wide = (jnp.dot(hi, expand, preferred_element_type=F32) + jnp.dot(lo, expand, preferred_element_type=F32))
            term = wide * _interleave_rows(r, scrs.pop(0) if d > 1 else None, d)
            o = term if o is None else o + term
    z = z_ref[...].astype(F32)
    gated = o * (z * _sigmoid(z))
    sub = jnp.dot(gated.astype(BF16), wout_ref[...], preferred_element_type=F32)
    out = _post_layer(x_ref[...], sub, p_ref[...], g_ref, b_ref, wpg_ref, wpe_ref, alpha)
    xo_ref[...] = out
    xb_ref[...] = out.astype(BF16)


def _tok_spec(tm, width, col=0):
    return pl.BlockSpec((tm, width), lambda i: (i, col))


def _full_spec(shape):
    return pl.BlockSpec(shape, lambda i: (0,) * len(shape))


def _post_outputs(t, tm):
    return dict(
        out_specs=[_tok_spec(tm, D_MODEL), _tok_spec(tm, D_MODEL)],
        out_shape=[jax.ShapeDtypeStruct((t, D_MODEL), F32), jax.ShapeDtypeStruct((t, D_MODEL), BF16)],
        compiler_params=pltpu.CompilerParams(dimension_semantics=("parallel",), vmem_limit_bytes=VMEM_LIMIT),
    )


def _glu_post(yg, z, x, p, layer, ln_g, ln_b, w_glu, w_out, w_pg, w_pe, alpha):
    t = x.shape[0]
    tm = _token_tile(t, 512)
    ple = p.shape[2]
    n_grp, _, cw = yg.shape
    chunk = cw // SSM_GROUP
    return pl.pallas_call(
        functools.partial(_glu_post_kernel, alpha=alpha, chunk=chunk), grid=(t // tm,),
        in_specs=[pl.BlockSpec((n_grp, tm // chunk, cw), lambda i: (0, i, 0)), _tok_spec(tm, D_MODEL),
                  _tok_spec(tm, D_MODEL), pl.BlockSpec((None, tm, ple), lambda i: (layer, i, 0)),
                  _full_spec((1, D_MODEL)), _full_spec((1, D_MODEL)), _full_spec((D_MODEL, D_MODEL)),
                  _full_spec((D_MODEL, D_MODEL)), _full_spec((D_MODEL, D_MODEL)), _full_spec((ple, D_MODEL))],
        scratch_shapes=[pltpu.VMEM((D_MODEL // LANES, tm, LANES), F32),
                        pltpu.VMEM((DEINT_STAGE, tm // DEINT_STAGE, LANES), F32)],
        name="glu_post", **_post_outputs(t, tm),
    )(yg, z, x, p, ln_g, ln_b, w_glu, w_out, w_pg, w_pe)


def _comb_post(os_, lses, dils, z, z_col, x, p, layer, ln_g, ln_b, w_out, w_pg, w_pe, alpha):
    t = x.shape[0]
    tm = _token_tile(t, 512)
    ple = p.shape[2]
    if len(dils) == 1:
        group_specs = [_tok_spec(tm, D_MODEL)]
        scratch = []
    else:
        group_specs = [pl.BlockSpec((d, tm // d, D_MODEL), lambda i: (0, i, 0)) for d in dils]
        group_specs += [pl.BlockSpec((d, tm // d, LANES), lambda i: (0, i, 0)) for d in dils]
        scratch = [pltpu.VMEM((1, tm, LANES), F32) for d in dils if d > 1]
        scratch += [pltpu.VMEM((D_MODEL // LANES, tm, LANES), F32) for d in dils if d > 1]
    return pl.pallas_call(
        functools.partial(_comb_post_kernel, dils=tuple(dils), alpha=alpha), grid=(t // tm,),
        in_specs=group_specs
        + [_tok_spec(tm, D_MODEL, z_col), _tok_spec(tm, D_MODEL), pl.BlockSpec((None, tm, ple), lambda i: (layer, i, 0)),
           _full_spec((1, D_MODEL)), _full_spec((1, D_MODEL)), _full_spec((D_MODEL, D_MODEL)),
           _full_spec((D_MODEL, D_MODEL)), _full_spec((ple, D_MODEL))],
        scratch_shapes=scratch, name="attn_out_post", **_post_outputs(t, tm),
    )(*os_, *lses, z, x, p, ln_g, ln_b, w_out, w_pg, w_pe)


def _attn_prompt_kernel(q_ref, kp_ref, kc_ref, vp_ref, vc_ref, o_ref, lse_ref):
    ib = pl.program_id(2)
    tq = q_ref.shape[0]
    row = lax.broadcasted_iota(jnp.int32, (tq, 2 * tq), 0)
    col = lax.broadcasted_iota(jnp.int32, (tq, 2 * tq), 1)
    back = row + tq - col
    valid = (back >= 0) & (back <= WINDOW_KEYS) & ((col >= tq) | (ib > 0))
    lane = lax.broadcasted_iota(jnp.int32, (tq, LANES), 1)
    low = lane < HEAD_DIM
    head_of_lane = lane >> ((LANES // N_HEADS).bit_length() - 1)
    valid2 = jnp.concatenate([valid, valid], axis=0)
    m_tile = jnp.zeros((tq, LANES), F32)
    l_tile = jnp.ones((tq, LANES), F32)
    nt = (((1,), (1,)), ((), ()))
    for j in range(ATTN_WIDTH // LANES):
        sl = slice(j * LANES, (j + 1) * LANES)
        q2 = q_ref[:, sl]
        k2 = jnp.concatenate([kp_ref[:, sl], kc_ref[:, sl]], axis=0)
        v2 = jnp.concatenate([vp_ref[:, sl], vc_ref[:, sl]], axis=0)
        zero = jnp.zeros_like(q2)
        qs = jnp.concatenate([jnp.where(low, q2, zero), jnp.where(low, zero, q2)], axis=0)
        s = lax.dot_general(qs, k2, nt, preferred_element_type=F32)
        s = jnp.where(valid2, s, NEG_INF)
        m = jnp.max(s, axis=1, keepdims=True)
        pr = jnp.exp2(s - m)
        l = jnp.sum(pr, axis=1, keepdims=True)
        o = jnp.dot(pr.astype(BF16), v2, preferred_element_type=F32) / l
        o_ref[:, sl] = jnp.where(low, o[:tq], o[tq:]).astype(o_ref.dtype)
        for hh in range(2):
            pick = head_of_lane == 2 * j + hh
            m_tile = jnp.where(pick, m[hh * tq:(hh + 1) * tq], m_tile)
            l_tile = jnp.where(pick, l[hh * tq:(hh + 1) * tq], l_tile)
    lse_ref[...] = m_tile + jnp.log2(l_tile)


def _attn_prompt_group(q, k, v, n_seq, seq_len):
    dil, rows, _ = q.shape
    tq = WINDOW_KEYS
    nb = seq_len // dil // tq
    assert rows == n_seq * nb * tq

    def cur(b, r, ib):
        return (r, b * nb + ib, 0)

    def prev(b, r, ib):
        return (r, b * nb + jnp.maximum(ib - 1, 0), 0)

    blk = (None, tq, D_MODEL)
    return pl.pallas_call(
        _attn_prompt_kernel, grid=(n_seq, dil, nb),
        in_specs=[pl.BlockSpec(blk, cur), pl.BlockSpec(blk, prev), pl.BlockSpec(blk, cur),
                  pl.BlockSpec(blk, prev), pl.BlockSpec(blk, cur)],
        out_specs=[pl.BlockSpec(blk, cur), pl.BlockSpec((None, tq, LANES), cur)],
        out_shape=[jax.ShapeDtypeStruct((dil, rows, D_MODEL), BF16), jax.ShapeDtypeStruct((dil, rows, LANES), F32)],
        compiler_params=pltpu.CompilerParams(dimension_semantics=("parallel", "parallel", "arbitrary"),
                                             vmem_limit_bytes=VMEM_LIMIT),
        name=f"attn_prompt_d{dil}",
    )(q, k, k, v, v)


SAMPLE_KEY_CHUNK = 512


def _attn_sample_kernel(q_ref, kvn_ref, c0_ref, c1_ref, c2_ref, o_ref, s_scr, m_scr):
    step = pl.program_id(1)
    ds = q_ref.shape[1]
    n_rows = N_HEADS * ds
    assert ds & (ds - 1) == 0 and n_rows == LANES
    rho = lax.broadcasted_iota(jnp.int32, (n_rows, ATTN_WIDTH), 0)
    lane = lax.broadcasted_iota(jnp.int32, (n_rows, ATTN_WIDTH), 1)
    head_mask = (rho >> (ds.bit_length() - 1)) == (lane >> (HEAD_DIM.bit_length() - 1))
    cache_refs = (c0_ref, c1_ref, c2_ref)
    nt = (((1,), (1,)), ((), ()))
    kvn = kvn_ref[0].astype(F32)

    def load_chunk(g, c0, cw):
        return cache_refs[g][0, 0, :, c0:c0 + cw].astype(BF16)

    def cache_valid(g, c0, cw):
        qi = lax.broadcasted_iota(jnp.int32, (n_rows, cw), 0) & (ds - 1)
        back = lax.broadcasted_iota(jnp.int32, (n_rows, cw), 1) + c0 - qi
        return (back >= 0) & ((back & (DILATIONS[g] - 1)) == 0)

    def new_block(g, off):
        piece = kvn[:, off + g * ATTN_WIDTH: off + (g + 1) * ATTN_WIDTH]
        pad = jnp.zeros((LANES - ds, ATTN_WIDTH), F32)
        return jnp.concatenate([piece, pad], axis=0).astype(BF16)

    chunks = []
    col = 0
    for g, c_ref in enumerate(cache_refs):
        width = c_ref.shape[3]
        for c0 in range(0, width, SAMPLE_KEY_CHUNK):
            cw = min(SAMPLE_KEY_CHUNK, width - c0)
            chunks.append((g, c0, cw, col))
            col += cw
    new_cols = [col + g * LANES for g in range(N_ATT_GROUPS)]

    @pl.when(step == 0)
    def _():
        q_all = q_ref[0].astype(F32)
        q_rows = []
        for g in range(N_ATT_GROUPS):
            qg = q_all[:, g * ATTN_WIDTH:(g + 1) * ATTN_WIDTH]
            q_rows.append(jnp.where(head_mask, jnp.concatenate([qg] * N_HEADS, axis=0), 0.0).astype(BF16))
        m = jnp.full((n_rows, 1), NEG_INF, F32)
        for g, c0, cw, col0 in chunks:
            s = jnp.dot(q_rows[g], load_chunk(g, c0, cw), preferred_element_type=F32)
            s = jnp.where(cache_valid(g, c0, cw), s, NEG_INF)
            s_scr[:, col0:col0 + cw] = s
            m = jnp.maximum(m, jnp.max(s, axis=1, keepdims=True))
        qi = lax.broadcasted_iota(jnp.int32, (n_rows, LANES), 0) & (ds - 1)
        kappa = lax.broadcasted_iota(jnp.int32, (n_rows, LANES), 1)
        for g in range(N_ATT_GROUPS):
            s = lax.dot_general(q_rows[g], new_block(g, 0), nt, preferred_element_type=F32)
            ahead = qi - kappa
            s = jnp.where((ahead >= 0) & ((ahead & (DILATIONS[g] - 1)) == 0) & (kappa < ds), s, NEG_INF)
            s_scr[:, new_cols[g]:new_cols[g] + LANES] = s
            m = jnp.maximum(m, jnp.max(s, axis=1, keepdims=True))
        m_scr[...] = jnp.broadcast_to(m, m_scr.shape)

    @pl.when(step == 1)
    def _():
        m = m_scr[:, 0:1]
        l = jnp.zeros((n_rows, 1), F32)
        acc = jnp.zeros((n_rows, ATTN_WIDTH), F32)
        for g, c0, cw, col0 in chunks:
            pr = jnp.exp2(s_scr[:, col0:col0 + cw] - m)
            l = l + jnp.sum(pr, axis=1, keepdims=True)
            acc = acc + lax.dot_general(pr.astype(BF16), load_chunk(g, c0, cw), nt, preferred_element_type=F32)
        for g in range(N_ATT_GROUPS):
            pr = jnp.exp2(s_scr[:, new_cols[g]:new_cols[g] + LANES] - m)
            l = l + jnp.sum(pr, axis=1, keepdims=True)
            acc = acc + jnp.dot(pr.astype(BF16), new_block(g, N_ATT_GROUPS * ATTN_WIDTH),
                                preferred_element_type=F32)
        acc = acc / l
        acc3 = acc.reshape(N_HEADS, ds, ATTN_WIDTH)
        low = lax.broadcasted_iota(jnp.int32, (ds, LANES), 1) < HEAD_DIM
        for c in range(ATTN_WIDTH // LANES):
            sl = slice(c * LANES, (c + 1) * LANES)
            o_ref[0, :, sl] = jnp.where(low, acc3[2 * c][:, sl], acc3[2 * c + 1][:, sl])


def _cache_views(caches):
    views = []
    for cache, dil in zip(caches, DILATIONS):
        n, width = cache.shape[:2]
        assert width == WINDOW_KEYS * dil and cache.shape[2:] == (2, N_HEADS, HEAD_DIM)
        views.append(cache.transpose(0, 2, 3, 4, 1).reshape(n, 2, ATTN_WIDTH, width))
    return views


def _attn_sample(qz_s, kv_s, views, n_seq, ds):
    q3 = qz_s.reshape(n_seq, ds, qz_s.shape[1])
    kv3 = kv_s.reshape(n_seq, ds, kv_s.shape[1])
    n_cols = sum(v.shape[3] for v in views) + N_ATT_GROUPS * LANES

    def tok_spec(width):
        return pl.BlockSpec((1, ds, width), lambda n, s: (n, 0, 0))

    def cache_spec(width):
        return pl.BlockSpec((1, 1, ATTN_WIDTH, width), lambda n, s: (n, s, 0, 0))

    o = pl.pallas_call(
        _attn_sample_kernel, grid=(n_seq, 2),
        in_specs=[tok_spec(q3.shape[2]), tok_spec(kv3.shape[2])] + [cache_spec(v.shape[3]) for v in views],
        out_specs=tok_spec(ATTN_WIDTH), out_shape=jax.ShapeDtypeStruct((n_seq, ds, ATTN_WIDTH), F32),
        scratch_shapes=[pltpu.VMEM((N_HEADS * ds, n_cols), F32), pltpu.VMEM((N_HEADS * ds, LANES), F32)],
        compiler_params=pltpu.CompilerParams(dimension_semantics=("parallel", "arbitrary"),
                                             vmem_limit_bytes=VMEM_LIMIT),
        name="attn_sample",
    )(q3, kv3, *views)
    return o.reshape(n_seq * ds, ATTN_WIDTH)


def kernel(x_prompt, x_sample, state_ssm_re, state_ssm_im, cache_kv_w128, cache_kv_w512, cache_kv_w2048, p_prompt, p_sample, ln_g, ln_b, w_pe, w_pg, w_in_a, a_re, a_im, log_dt, b_re, b_im, c_re, c_im, d_skip, w_glu, w_out_a, w_kv, w_in_b, w_out_b):
    depth = ln_g.shape[0]
    n_a = w_in_a.shape[0]
    batch, seq, _ = x_prompt.shape
    n_s, ds, _ = x_sample.shape
    past = cache_kv_w2048.shape[1]
    assert ds * 2 == SSM_CHUNK and ds <= DILATIONS[2] and past == WINDOWS[2] and seq % (DILATIONS[2] * WINDOW_KEYS) == 0
    alpha = (2 * depth) ** 0.25
    t_p, t_s = batch * seq, n_s * ds
    caches =(cache_kv_w128, cache_kv_w512, cache_kv_w2048)

    xs = [x_prompt.reshape(t_p, D_MODEL), x_sample.reshape(t_s, D_MODEL)]
    xbs = list(xs)
    ps = [p_prompt.reshape(depth, t_p, -1), p_sample.reshape(depth, t_s, -1)]
    wb = lambda w: w.astype(BF16)
    h_prompt, h_sample = [], []

    for i in range(n_a):
        w_in = wb(w_in_a[i])
        (xg, z_p), (xsg, z_s) = _project_ssm(xbs[0], w_in, SSM_CHUNK), _project_ssm(xbs[1], w_in, ds)
        ops = _ssm_operators(a_re[i], a_im[i], log_dt[i], b_re[i], b_im[i], c_re[i], c_im[i], d_skip[i])
        h0g = jnp.concatenate([state_ssm_re[i], state_ssm_im[i]], axis=-1).astype(F32).transpose(1, 0, 2)
        yg, hl, ysg, hs = _ssm(xg, xsg, h0g, ops, n_seq=batch)
        h_prompt.append(hl.transpose(1, 0, 2))
        h_sample.append(hs.transpose(1, 0, 2))
        for k, (y, z) in enumerate(((yg, z_p), (ysg, z_s))):
            xs[k], xbs[k] = _glu_post(y, z, xs[k], ps[k], i, ln_g[i:i + 1], ln_b[i:i + 1], wb(w_glu[i]),
                                      wb(w_out_a[i]), wb(w_pg[i]), wb(w_pe[i]), alpha)

    pos = [jnp.arange(seq, dtype=F32), jnp.tile(past + jnp.arange(ds, dtype=F32), n_s)]
    rope_k = [_rope_tables(p_, 1.0) for p_ in pos]
    q_scale = HEAD_DIM ** -0.5 * math.log2(math.e)
    rope_q = [tuple(q_scale * tab for tab in tabs) for tabs in rope_k]
    q_width = N_ATT_GROUPS * ATTN_WIDTH
    n_grp = N_ATT_GROUPS
    w_kv_b = wb(w_kv)
    k_p = [_project_deint(xbs[0], w_kv_b, grp, DILATIONS[grp], rope_k[0]) for grp in range(n_grp)]
    v_p = [_project_deint(xbs[0], w_kv_b, n_grp + grp, DILATIONS[grp]) for grp in range(n_grp)]
    kv_s = _project(xbs[1], w_kv_b, rope=rope_k[1], rot_cols=q_width, tn=512)
    views = _cache_views(caches)

    for i in range(n_a, depth):
        j = i - n_a
        w_in = wb(w_in_b[j])
        post_w = (ln_g[i:i + 1], ln_b[i:i + 1], wb(w_out_b[j]), wb(w_pg[i]), wb(w_pe[i]), alpha)
        q_p = [_project_deint(xbs[0], w_in, grp, DILATIONS[grp], rope_q[0]) for grp in range(n_grp)]
        z_p = _project_deint(xbs[0], w_in, n_grp, 1).reshape(t_p, D_MODEL)
        parts = [_attn_prompt_group(q_p[grp], k_p[grp], v_p[grp], batch, seq) for grp in range(n_grp)]
        qz_s = _project(xbs[1], w_in, rope=rope_q[1], rot_cols=q_width, tn=512)
        o_s = _attn_sample(qz_s, kv_s, views, n_s, ds)
        xs[0], xbs[0] = _comb_post([pt[0] for pt in parts], [pt[1] for pt in parts], DILATIONS, z_p, 0,
                                   xs[0], ps[0], i, *post_w)
        xs[1], xbs[1] = _comb_post([o_s], [], (1,), qz_s, n_grp, xs[1], ps[1], i, *post_w)

    y_prompt = xs[0].reshape(batch, seq, D_MODEL)
    y_sample = xs[1].reshape(n_s, ds, D_MODEL)
    hp = jnp.stack(h_prompt)
    hsm = jnp.stack(h_sample)

    def prompt_window(arr, dil, width):
        per = arr.reshape(dil, batch, seq // dil, ATTN_WIDTH)[:, :, (seq - width) // dil:]
        return per.transpose(1, 2, 0, 3).reshape(batch, width, N_HEADS, HEAD_DIM).astype(F32)

    kv_prompt = [jnp.stack([prompt_window(k_p[grp], DILATIONS[grp], min(w, seq)),
                            prompt_window(v_p[grp], DILATIONS[grp], min(w, seq))], axis=2)
                 for grp, w in enumerate(WINDOWS)]
    kv_s6 = kv_s.astype(F32).reshape(n_s, ds, 2, N_ATT_GROUPS, N_HEADS, HEAD_DIM)
    kv_sample = [kv_s6[:, :, :, grp] for grp in range(N_ATT_GROUPS)]
    return (y_prompt, y_sample, hp[..., :SSM_STATE], hp[..., SSM_STATE:], hsm[..., :SSM_STATE], hsm[..., SSM_STATE:],
            *kv_prompt, *kv_sample)
```
